```python
import math
import jax, jax.numpy as jnp
from jax import lax
import numpy as np

D_MODEL = 1024
BATCH = 8
SEQ = 8192
DEPTH = 1
DEC_BATCH = 16
DEC_SEQ = 4096
PAST_LEN = 128

HEAD_DIM = 64
A_HEADS_PER_GROUP = 4
DILATED_GROUPS = ((128, 1), (512, 4), (2048, 16))
A_HEADS = A_HEADS_PER_GROUP * len(DILATED_GROUPS)
NA_HEADS = 8
N_HEADS_TOTAL = A_HEADS + NA_HEADS
MIX_WIDTH = N_HEADS_TOTAL * HEAD_DIM
A_OUT_WIDTH = A_HEADS_PER_GROUP * HEAD_DIM
B_WIDTH = NA_HEADS * HEAD_DIM
DIL_BLOCK = 64
T5_BUCKETS = 32
T5_MAX_DISTANCE = 1024
GRID_W = 64
NA_KH = 8
NA_KW = 16
N_EXPERTS = 32
TOP_K = 4
D_EXPERT = 1024
EXPERT_BLOCK = 256
SWIGLU_LIMIT = 7.0
SWIGLU_ALPHA = 1.702
RMS_EPS = 1e-6
NEG = -1e30

kernel_name = "hybrid_dilated_neighbourhood_moe_encoder"


def rms_norm(x, g):
    xf = x.astype(jnp.float32)
    y = xf * lax.rsqrt(jnp.mean(xf * xf, axis=-1, keepdims=True) + RMS_EPS) * g.astype(jnp.float32)
    return y.astype(x.dtype)


def t5_bucket(rel):
    nb = T5_BUCKETS // 2
    ret = (rel > 0).astype(np.int32) * nb
    n = np.abs(rel)
    max_exact = nb // 2
    large = max_exact + (np.log(np.maximum(n, 1) / max_exact)
                         / np.log(T5_MAX_DISTANCE / max_exact) * (nb - max_exact)).astype(np.int32)
    large = np.minimum(large, nb - 1)
    return ret + np.where(n < max_exact, n, large)


def t5_bias_block(table, dilation):
    i = np.arange(DIL_BLOCK)[:, None]
    j = np.arange(3 * DIL_BLOCK)[None, :]
    buckets = t5_bucket(dilation * (j - DIL_BLOCK - i))
    return jnp.transpose(table[buckets], (2, 0, 1)).astype(jnp.float32)


def dilated_attention(q, k, v, bias, dilation, half):
    B, T, H, dh = q.shape
    L = T // dilation
    nb = -(-L // DIL_BLOCK)
    Lp = nb * DIL_BLOCK

    def classes(x):
        return x.reshape(B, L, dilation, H, dh).transpose(0, 2, 1, 3, 4)

    qb = jnp.pad(classes(q), ((0, 0), (0, 0), (0, Lp - L), (0, 0), (0, 0)))
    qb = qb.reshape(B, dilation, nb, DIL_BLOCK, H, dh)

    def key_windows(x):
        xb = jnp.pad(classes(x), ((0, 0), (0, 0), (DIL_BLOCK, Lp - L + DIL_BLOCK), (0, 0), (0, 0)))
        xb = xb.reshape(B, dilation, nb + 2, DIL_BLOCK, H, dh)
        return jnp.concatenate([xb[:, :, :-2], xb[:, :, 1:-1], xb[:, :, 2:]], axis=3)

    kw = key_windows(k)
    vw = key_windows(v)
    i = np.arange(DIL_BLOCK)[:, None]
    j = np.arange(3 * DIL_BLOCK)[None, :]
    rel = j - DIL_BLOCK - i
    m_k = (np.arange(nb)[:, None, None] - 1) * DIL_BLOCK + j[None]
    valid = (np.abs(rel)[None] <= half) & (m_k >= 0) & (m_k < L)
    s = jnp.einsum('bgnqhe,bgnkhe->bgnhqk', qb, kw, preferred_element_type=jnp.float32)
    s = s * (dh ** -0.5) + bias[None, None, None]
    s = jnp.where(valid[None, None, :, None], s, NEG)
    lse = jax.nn.logsumexp(s, axis=-1, keepdims=True)
    p = jnp.exp(s - lse)
    o = jnp.einsum('bgnhqk,bgnkhe->bgnqhe', p, vw, preferred_element_type=jnp.float32)
    o = o.reshape(B, dilation, Lp, H, dh)[:, :, :L].transpose(0, 2, 1, 3, 4).reshape(B, T, H, dh)
    lse = lse[..., 0].transpose(0, 1, 2, 4, 3).reshape(B, dilation, Lp, H)[:, :, :L]
    lse = lse.transpose(0, 2, 1, 3).reshape(B, T, H)
    return o, lse


def neighborhood_attention(q, k, v, rpb):
    B, T, H, dh = q.shape
    rows = T // GRID_W
    kh = min(NA_KH, rows)
    n_cb = GRID_W // NA_KW
    q_cols = np.arange(GRID_W).reshape(n_cb, NA_KW)
    key_start = np.clip(np.arange(n_cb) * NA_KW - NA_KW // 2, 0, GRID_W - 2 * NA_KW)
    key_cols = key_start[:, None] + np.arange(2 * NA_KW)
    win_start = np.clip(q_cols - NA_KW // 2, 0, GRID_W - NA_KW)
    kc = key_cols[:, None, :]
    ws = win_start[:, :, None]
    col_mask = (kc >= ws) & (kc < ws + NA_KW)
    col_idx = np.clip(kc - q_cols[:, :, None] + NA_KW - 1, 0, 2 * NA_KW - 2)
    qg = q.reshape(B, rows, n_cb, NA_KW, H, dh)
    kg = k.reshape(B, rows, GRID_W, H, dh)
    vg = v.reshape(B, rows, GRID_W, H, dh)
    scale = dh ** -0.5

    def one_row(r):
        rs = jnp.clip(r - kh // 2, 0, rows - kh)
        k_rows = lax.dynamic_slice_in_dim(kg, rs, kh, axis=1)[:, :, key_cols]
        v_rows = lax.dynamic_slice_in_dim(vg, rs, kh, axis=1)[:, :, key_cols]
        q_row = lax.dynamic_index_in_dim(qg, r, axis=1, keepdims=False)
        row_idx = rs + jnp.arange(kh) - r + NA_KH - 1
        bias = rpb[:, row_idx[None, None, :, None], col_idx[:, :, None, :]]
        s = jnp.einsum('bcqhd,brckhd->bhcqrk', q_row, k_rows, preferred_element_type=jnp.float32)
        s = s * scale + bias.astype(jnp.float32)
        s = jnp.where(col_mask[:, :, None, :], s, NEG)
        p = jax.nn.softmax(s.reshape(B, H, n_cb, NA_KW, kh * 2 * NA_KW), axis=-1).reshape(s.shape)
        o = jnp.einsum('bhcqrk,brckhd->bcqhd', p, v_rows, preferred_element_type=jnp.float32)
        return o.reshape(B, GRID_W, H, dh)

    out = lax.map(one_row, jnp.arange(rows))
    return out.transpose(1, 0, 2, 3, 4).reshape(B, T, H, dh)


def moe_ffn(h, w_router, b_router, w_gate_up, b_gate_up, w_down, b_down):
    N, D = h.shape
    logits = jnp.dot(h, w_router, preferred_element_type=jnp.float32) + b_router.astype(jnp.float32)
    top_vals, top_idx = lax.top_k(logits, TOP_K)
    gates = jax.nn.softmax(top_vals, axis=-1)
    NK = N * TOP_K
    flat_e = top_idx.reshape(-1).astype(jnp.int32)
    flat_tok = jnp.arange(NK, dtype=jnp.int32) // TOP_K
    flat_g = gates.reshape(-1)
    order = jnp.argsort(flat_e)
    e_sorted, tok_sorted, g_sorted = flat_e[order], flat_tok[order], flat_g[order]
    counts = jnp.bincount(flat_e, length=N_EXPERTS).astype(jnp.int32)
    padded = (counts + EXPERT_BLOCK - 1) // EXPERT_BLOCK * EXPERT_BLOCK
    start = jnp.cumsum(counts) - counts
    pend = jnp.cumsum(padded)
    pstart = pend - padded
    dest = pstart[e_sorted] + (jnp.arange(NK, dtype=jnp.int32) - start[e_sorted])
    P = NK + N_EXPERTS * EXPERT_BLOCK
    n_blocks = P // EXPERT_BLOCK
    slot_tok = jnp.zeros((P,), jnp.int32).at[dest].set(tok_sorted)
    slot_gate = jnp.zeros((P,), jnp.float32).at[dest].set(g_sorted)
    block_e = jnp.searchsorted(pend, jnp.arange(n_blocks, dtype=jnp.int32) * EXPERT_BLOCK, side='right')
    block_e = jnp.minimum(block_e, N_EXPERTS - 1).astype(jnp.int32)

    def body(y, blk):
        tok, g, e = blk
        xb = h[tok]
        gu = jnp.dot(xb, w_gate_up[e], preferred_element_type=jnp.float32) + b_gate_up[e].astype(jnp.float32)
        x_glu = jnp.minimum(gu[:, :D_EXPERT], SWIGLU_LIMIT)
        x_lin = jnp.clip(gu[:, D_EXPERT:], -SWIGLU_LIMIT, SWIGLU_LIMIT)
        act = x_glu * jax.nn.sigmoid(SWIGLU_ALPHA * x_glu) * (x_lin + 1.0)
        out = jnp.dot(act.astype(w_down.dtype), w_down[e], preferred_element_type=jnp.float32)
        out = out + b_down[e].astype(jnp.float32)
        return y.at[tok].add(out * g[:, None]), None

    y0 = jnp.zeros((N, D), jnp.float32)
    y, _ = lax.scan(body, y0, (slot_tok.reshape(n_blocks, EXPERT_BLOCK),
                               slot_gate.reshape(n_blocks, EXPERT_BLOCK), block_e))
    return y.astype(h.dtype)


def encoder_layer(x, t5_rel_bias, norm_mix, w_in, na_rpb, w_branch_a, w_branch_b, w_gate, w_out,
                  norm_ffn, w_router, b_router, w_gate_up, b_gate_up, w_down, b_down):
    B, T, D = x.shape
    h = rms_norm(x, norm_mix)
    qkv = jnp.einsum('btd,de->bte', h, w_in).reshape(B, T, 3, N_HEADS_TOTAL, HEAD_DIM)
    q, k, v = qkv[:, :, 0], qkv[:, :, 1], qkv[:, :, 2]

    outs, lses = [], []
    for g, (window, dilation) in enumerate(DILATED_GROUPS):
        hs = slice(g * A_HEADS_PER_GROUP, (g + 1) * A_HEADS_PER_GROUP)
        bias = t5_bias_block(t5_rel_bias[:, hs], dilation)
        o, lse = dilated_attention(q[:, :, hs], k[:, :, hs], v[:, :, hs], bias, dilation,
                                   window // (2 * dilation))
        outs.append(o)
        lses.append(lse)
    wts = jax.nn.softmax(jnp.stack(lses, axis=0), axis=0)[..., None]
    o_a = jnp.sum(wts * jnp.stack(outs, axis=0), axis=0).reshape(B, T, A_OUT_WIDTH).astype(x.dtype)

    o_b = neighborhood_attention(q[:, :, A_HEADS:], k[:, :, A_HEADS:], v[:, :, A_HEADS:], na_rpb)
    o_b = o_b.reshape(B, T, B_WIDTH).astype(x.dtype)

    gates = jax.nn.sigmoid(jnp.einsum('btd,de->bte', h, w_gate).astype(jnp.float32)).reshape(B, T, 2, D)
    br_a = jnp.einsum('btc,cd->btd', o_a, w_branch_a).astype(jnp.float32)
    br_b = jnp.einsum('btc,cd->btd', o_b, w_branch_b).astype(jnp.float32)
    merged = (gates[:, :, 0] * br_a + gates[:, :, 1] * br_b).astype(x.dtype)
    x = x + jnp.einsum('btd,de->bte', merged, w_out)

    h2 = rms_norm(x, norm_ffn).reshape(B * T, D)
    x = x + moe_ffn(h2, w_router, b_router, w_gate_up, b_gate_up, w_down, b_down).reshape(B, T, D)
    return x


def setup_inputs(seed: int = 0) -> dict:
    key = jax.random.key(seed)
    ks = jax.random.split(key, 18)

    def normal(k, shape, scale):
        return jax.random.normal(k, shape, jnp.float32) * scale

    return {
        "x_prompt": normal(ks[0], (BATCH, SEQ, D_MODEL), 1.0),
        "x_sample": normal(ks[1], (DEC_BATCH, DEC_SEQ, D_MODEL), 1.0),
        "t5_rel_bias": normal(ks[2], (T5_BUCKETS, A_HEADS), 0.5),
        "norm_mix": 1.0 + normal(ks[3], (DEPTH, D_MODEL), 0.02),
        "w_in": normal(ks[4], (DEPTH, D_MODEL, 3 * MIX_WIDTH), D_MODEL ** -0.5),
        "na_rpb": normal(ks[5], (DEPTH, NA_HEADS, 2 * NA_KH - 1, 2 * NA_KW - 1), 0.5),
        "w_branch_a": normal(ks[6], (DEPTH, A_OUT_WIDTH, D_MODEL), A_OUT_WIDTH ** -0.5),
        "w_branch_b": normal(ks[7], (DEPTH, B_WIDTH, D_MODEL), B_WIDTH ** -0.5),
        "w_gate": normal(ks[8], (DEPTH, D_MODEL, 2 * D_MODEL), D_MODEL ** -0.5),
        "w_out": normal(ks[9], (DEPTH, D_MODEL, D_MODEL), D_MODEL ** -0.5),
        "norm_ffn": 1.0 + normal(ks[10], (DEPTH, D_MODEL), 0.02),
        "w_router": normal(ks[11], (DEPTH, D_MODEL, N_EXPERTS), D_MODEL ** -0.5),
        "b_router": normal(ks[12], (DEPTH, N_EXPERTS), 0.01),
        "w_gate_up": normal(ks[13], (DEPTH, N_EXPERTS, D_MODEL, 2 * D_EXPERT), D_MODEL ** -0.5),
        "b_gate_up": normal(ks[14], (DEPTH, N_EXPERTS, 2 * D_EXPERT), 0.01),
        "w_down": normal(ks[15], (DEPTH, N_EXPERTS, D_EXPERT, D_MODEL), D_EXPERT ** -0.5),
        "b_down": normal(ks[16], (DEPTH, N_EXPERTS, D_MODEL), 0.01),
        "norm_final": 1.0 + normal(ks[17], (D_MODEL,), 0.02),
    }


def reference(x_prompt, x_sample, t5_rel_bias, norm_mix, w_in, na_rpb, w_branch_a, w_branch_b,
              w_gate, w_out, norm_ffn, w_router, b_router, w_gate_up, b_gate_up, w_down, b_down,
              norm_final):
    def encode(x):
        for l in range(DEPTH):
            x = encoder_layer(x, t5_rel_bias, norm_mix[l], w_in[l], na_rpb[l], w_branch_a[l],
                              w_branch_b[l], w_gate[l], w_out[l], norm_ffn[l], w_router[l],
                              b_router[l], w_gate_up[l], b_gate_up[l], w_down[l], b_down[l])
        return rms_norm(x, norm_final)

    y_prompt = encode(x_prompt)
    y_sample = encode(x_sample)
    return (y_prompt, y_sample)
```

```python
import functools

import numpy as np
import jax
import jax.numpy as jnp
from jax import lax
from jax.experimental import pallas as pl
from jax.experimental.pallas import tpu as pltpu

F32 = jnp.float32
BF16 = jnp.bfloat16
I32 = jnp.int32
U32 = jnp.uint32

HEAD_DIM = 64
QUAD_HEADS = 4
QUAD = QUAD_HEADS * HEAD_DIM
QKV_W = 3 * QUAD
DILATED_GROUPS = ((128, 1), (512, 4), (2048, 16))
N_DIL = len(DILATED_GROUPS)
NA_QUADS = 2
N_QUADS = N_DIL + NA_QUADS
BAND_HALF = 64
T5_BUCKETS = 32
T5_MAX_DISTANCE = 1024
GRID_W = 64
NA_KH = 8
NA_KW = 16
N_EXPERTS = 32
TOP_K = 4
SWIGLU_LIMIT = 7.0
SWIGLU_ALPHA = 1.702
RMS_EPS = 1e-6
NEG = -1e30

BAND_BQ = 128
NA_BAND_ROWS = 8
TM_QKV = 512
TM_MERGE = 256
TM_DISPATCH = 512
EXPERT_BM = 256
SEG_ALIGN = 8
VMEM_LIMIT = 56 * 1024 * 1024


def _round_up(x, m):
    return (x + m - 1) // m * m


def _cparams(*sem):
    return pltpu.CompilerParams(dimension_semantics=sem, vmem_limit_bytes=VMEM_LIMIT)


def _rms(x, g):
    ms = jnp.mean(x * x, axis=-1, keepdims=True)
    return x * lax.rsqrt(ms + RMS_EPS) * g


def _select_group(w, metas, fn):
    out = fn(metas[0], w - metas[0]["base"])
    for m in metas[1:]:
        cand = fn(m, w - m["base"])
        sel = w >= m["base"]
        out = jax.tree.map(lambda a, b: jnp.where(sel, b, a), out, cand)
    return out


def _qkv_kernel(x_ref, g_ref, w_ref, o_ref):
    h = _rms(x_ref[...], g_ref[...]).astype(BF16)
    for j in range(N_QUADS):
        o_ref[j] = jnp.dot(h, w_ref[:, j * QKV_W:(j + 1) * QKV_W],
                           preferred_element_type=F32).astype(BF16)


def _qkv_proj(x, g, w):
    n, d = x.shape
    tm = TM_QKV
    return pl.pallas_call(
        _qkv_kernel,
        grid=(n // tm,),
        in_specs=[pl.BlockSpec((tm, d), lambda i: (i, 0)),
                  pl.BlockSpec((1, d), lambda i: (0, 0)),
                  pl.BlockSpec((d, N_QUADS * QKV_W), lambda i: (0, 0))],
        out_specs=pl.BlockSpec((N_QUADS, tm, QKV_W), lambda i: (0, i, 0)),
        out_shape=jax.ShapeDtypeStruct((N_QUADS, n, QKV_W), BF16),
        compiler_params=_cparams("parallel"),
    )(x, g, w)


def _head_lane_masks(rows):
    lane_head = lax.broadcasted_iota(I32, (rows, QUAD), 1) // HEAD_DIM
    return [lane_head == h for h in range(QUAD_HEADS)]


def _stack_heads(q, masks):
    zero = jnp.zeros_like(q)
    return jnp.concatenate([jnp.where(m, q, zero) for m in masks], axis=0)


def _band_kernel(q_ref, kp_ref, kc_ref, kn_ref, vp_ref, vc_ref, vn_ref, b_ref, o_ref, l_ref):
    bq = q_ref.shape[0]
    masks = _head_lane_masks(bq)
    qs = _stack_heads(q_ref[...], masks)
    k = jnp.concatenate([kp_ref[...], kc_ref[...], kn_ref[...]], axis=0)
    v = jnp.concatenate([vp_ref[...], vc_ref[...], vn_ref[...]], axis=0)
    s = lax.dot_general(qs, k, (((1,), (1,)), ((), ())), preferred_element_type=F32) + b_ref[...]
    m = jnp.max(s, axis=-1, keepdims=True)
    p = jnp.exp(s - m)
    l = jnp.sum(p, axis=-1, keepdims=True)
    pv = jnp.dot(p.astype(BF16), v, preferred_element_type=F32) * (1.0 / l)
    lse = jnp.broadcast_to(m + jnp.log(l), pv.shape)
    o = jnp.zeros((bq, QUAD), F32)
    le = jnp.zeros((bq, QUAD), F32)
    for h in range(QUAD_HEADS):
        o = jnp.where(masks[h], pv[h * bq:(h + 1) * bq], o)
        le = jnp.where(masks[h], lse[h * bq:(h + 1) * bq], le)
    o_ref[...] = o
    l_ref[...] = le


def _t5_bucket(rel):
    nb = T5_BUCKETS // 2
    ret = (rel > 0).astype(np.int32) * nb
    n = np.abs(rel)
    max_exact = nb // 2
    large = max_exact + (np.log(np.maximum(n, 1) / max_exact)
                         / np.log(T5_MAX_DISTANCE / max_exact) * (nb - max_exact)).astype(np.int32)
    large = np.minimum(large, nb - 1)
    return ret + np.where(n < max_exact, n, large)


def _band_bias(table, dilation, bq):
    nk = bq + 2 * BAND_HALF
    qi = np.arange(bq)[:, None]
    c = np.arange(nk)[None, :]
    rel = c - BAND_HALF - qi
    band = np.abs(rel) <= BAND_HALF
    bias = jnp.transpose(table[_t5_bucket(dilation * rel)], (2, 0, 1)).astype(F32)
    out = []
    for variant in range(4):
        valid = band
        if variant & 1:
            valid = valid & (c >= BAND_HALF)
        if variant & 2:
            valid = valid & (c < bq + BAND_HALF)
        out.append(jnp.where(valid[None], bias, NEG).reshape(QUAD_HEADS * bq, nk))
    return jnp.stack(out, axis=0)


def _band_attention(qkv, group, dilation, bias, groups):
    _, n, _ = qkv.shape
    d, bq = dilation, BAND_BQ
    hq = bq // BAND_HALF
    a = qkv.reshape(N_QUADS, n // d, d * QKV_W)
    metas, base = [], 0
    for (b, t, tok0) in groups:
        seq = t // d
        assert seq % bq == 0 and tok0 % (d * bq) == 0
        nb = seq // bq
        metas.append(dict(base=base, nb=nb, rb0=tok0 // d // bq))
        base += b * d * nb
    n_items = base

    def decode(w):
        def f(m, wl):
            nb = m["nb"]
            s = wl // (d * nb)
            r = (wl // nb) % d
            i = wl % nb
            seq0 = m["rb0"] + s * nb
            return dict(r=r, i=i, seq0=seq0, last=nb - 1 + 0 * i)
        return _select_group(w, metas, f)

    def cur(col):
        def im(w):
            t = decode(w)
            return (group, t["seq0"] + t["i"], 3 * t["r"] + col)
        return pl.BlockSpec((None, bq, QUAD), im)

    def prev(col):
        def im(w):
            t = decode(w)
            return (group, hq * t["seq0"] + jnp.maximum(hq * t["i"] - 1, 0), 3 * t["r"] + col)
        return pl.BlockSpec((None, BAND_HALF, QUAD), im)

    def nxt(col):
        def im(w):
            t = decode(w)
            j = jnp.minimum(hq * t["i"] + hq, hq * t["last"] + hq - 1)
            return (group, hq * t["seq0"] + j, 3 * t["r"] + col)
        return pl.BlockSpec((None, BAND_HALF, QUAD), im)

    def bias_im(w):
        t = decode(w)
        variant = (t["i"] == 0).astype(I32) + 2 * (t["i"] == t["last"]).astype(I32)
        return (variant, 0, 0)

    def out_im(w):
        t = decode(w)
        return (t["seq0"] + t["i"], t["r"])

    nk = bq + 2 * BAND_HALF
    o, lse = pl.pallas_call(
        _band_kernel,
        grid=(n_items,),
        in_specs=[cur(0), prev(1), cur(1), nxt(1), prev(2), cur(2), nxt(2),
                  pl.BlockSpec((None, QUAD_HEADS * bq, nk), bias_im)],
        out_specs=[pl.BlockSpec((bq, QUAD), out_im), pl.BlockSpec((bq, QUAD), out_im)],
        out_shape=[jax.ShapeDtypeStruct((n // d, d * QUAD), F32)] * 2,
        compiler_params=_cparams("parallel"),
    )(a, a, a, a, a, a, a, bias)
    return o.reshape(n, QUAD), lse.reshape(n, QUAD)


def _na_bias(rpb):
    dv = np.arange(NA_KH)[:, None]
    kr = np.arange(NA_KH)[None, :]
    row_idx = dv + kr
    qc = np.arange(GRID_W)[:, None]
    kc = np.arange(GRID_W)[None, :]
    col_idx = np.clip(kc - qc + NA_KW - 1, 0, 2 * NA_KW - 2)
    ws = np.clip(qc - NA_KW // 2, 0, GRID_W - NA_KW)
    mask = (kc >= ws) & (kc < ws + NA_KW)
    b = rpb[:, row_idx[:, None, :, None], col_idx[None, :, None, :]].astype(F32)
    b = jnp.where(mask[None, None, :, None, :], b, NEG)
    b = b.reshape(NA_QUADS, QUAD_HEADS, NA_KH, GRID_W, NA_KH * GRID_W)
    return b.transpose(0, 2, 1, 3, 4).reshape(NA_QUADS, NA_KH, QUAD_HEADS * GRID_W, NA_KH * GRID_W)


def _na_kernel(q_ref, kp_ref, kc_ref, kn_ref, vp_ref, vc_ref, vn_ref, b_ref, o_ref,
               kbuf, vbuf, *, metas, items_per_quad):
    blk = NA_BAND_ROWS * GRID_W
    w = pl.program_id(0) % items_per_quad
    t = _select_group(w, metas, lambda m, wl: dict(i=wl % m["nbands"], nbands=m["nbands"] + 0 * wl))
    band, rows = t["i"], t["nbands"] * NA_BAND_ROWS
    kbuf[0:blk] = kp_ref[...]
    kbuf[blk:2 * blk] = kc_ref[...]
    kbuf[2 * blk:3 * blk] = kn_ref[...]
    vbuf[0:blk] = vp_ref[...]
    vbuf[blk:2 * blk] = vc_ref[...]
    vbuf[2 * blk:3 * blk] = vn_ref[...]
    masks = _head_lane_masks(GRID_W)

    def body(j, carry):
        r = band * NA_BAND_ROWS + j
        rs = jnp.clip(r - NA_KH // 2, 0, rows - NA_KH)
        loc = pl.multiple_of((rs - (band - 1) * NA_BAND_ROWS) * GRID_W, GRID_W)
        kw = kbuf[pl.ds(loc, NA_KH * GRID_W), :]
        vw = vbuf[pl.ds(loc, NA_KH * GRID_W), :]
        qrow = pl.ds(pl.multiple_of(j * GRID_W, GRID_W), GRID_W)
        qs = _stack_heads(q_ref[qrow, :], masks)
        s = lax.dot_general(qs, kw, (((1,), (1,)), ((), ())), preferred_element_type=F32)
        s = s + b_ref[rs - r + NA_KH - 1]
        m = jnp.max(s, axis=-1, keepdims=True)
        p = jnp.exp(s - m)
        l = jnp.sum(p, axis=-1, keepdims=True)
        pv = jnp.dot(p.astype(BF16), vw, preferred_element_type=F32) * (1.0 / l)
        o = jnp.zeros((GRID_W, QUAD), F32)
        for h in range(QUAD_HEADS):
            o = jnp.where(masks[h], pv[h * GRID_W:(h + 1) * GRID_W], o)
        o_ref[qrow, :] = o
        return carry

    lax.fori_loop(0, NA_BAND_ROWS, body, 0)


def _neighbourhood_attention(qkv, bias, groups):
    _, n, _ = qkv.shape
    blk = NA_BAND_ROWS * GRID_W
    metas, base = [], 0
    for (b, t, tok0) in groups:
        rows = t // GRID_W
        assert t % GRID_W == 0 and rows % NA_BAND_ROWS == 0 and rows >= NA_KH and tok0 % blk == 0
        nbands = rows // NA_BAND_ROWS
        metas.append(dict(base=base, nbands=nbands, tb0=tok0 // blk))
        base += b * nbands
    items_per_quad = base

    def decode(w):
        quad = w // items_per_quad
        wq = w % items_per_quad

        def f(m, wl):
            s = wl // m["nbands"]
            i = wl % m["nbands"]
            return dict(i=i, seq0=m["tb0"] + s * m["nbands"], last=m["nbands"] - 1 + 0 * i)
        t = _select_group(wq, metas, f)
        t["quad"] = quad
        return t

    def spec(col, shift):
        def im(w):
            t = decode(w)
            i = jnp.clip(t["i"] + shift, 0, t["last"])
            return (N_DIL + t["quad"], t["seq0"] + i, col)
        return pl.BlockSpec((None, blk, QUAD), im)

    def out_im(w):
        t = decode(w)
        return (t["seq0"] + t["i"], t["quad"])

    return pl.pallas_call(
        functools.partial(_na_kernel, metas=metas, items_per_quad=items_per_quad),
        grid=(NA_QUADS * items_per_quad,),
        in_specs=[spec(0, 0), spec(1, -1), spec(1, 0), spec(1, 1), spec(2, -1), spec(2, 0), spec(2, 1),
                  pl.BlockSpec((None, NA_KH, QUAD_HEADS * GRID_W, NA_KH * GRID_W),
                               lambda w: (w // items_per_quad, 0, 0, 0))],
        out_specs=pl.BlockSpec((blk, QUAD), out_im),
        out_shape=jax.ShapeDtypeStruct((n, NA_QUADS * QUAD), F32),
        scratch_shapes=[pltpu.VMEM((3 * blk, QUAD), BF16), pltpu.VMEM((3 * blk, QUAD), BF16)],
        compiler_params=_cparams("parallel"),
    )(qkv, qkv, qkv, qkv, qkv, qkv, qkv, bias)


def _merge_kernel(x_ref, o0_ref, l0_ref, o1_ref, l1_ref, o2_ref, l2_ref, ob_ref,
                  gmix_ref, wg_ref, wa_ref, wb_ref, wo_ref, gffn_ref, wrh_ref, wrl_ref, br_ref, tri_ref,
                  x1_ref, h2_ref, e_ref, gate_ref, q_ref, cnt_ref, carry_ref, *, sub_tiles):
    d = x_ref.shape[1]
    tm = x_ref.shape[0]
    x = x_ref[...]
    h = _rms(x, gmix_ref[...]).astype(BF16)
    gates = jax.nn.sigmoid(jnp.dot(h, wg_ref[...], preferred_element_type=F32))

    l0, l1, l2 = l0_ref[...], l1_ref[...], l2_ref[...]
    m = jnp.maximum(jnp.maximum(l0, l1), l2)
    e0, e1, e2 = jnp.exp(l0 - m), jnp.exp(l1 - m), jnp.exp(l2 - m)
    oa = (e0 * o0_ref[...] + e1 * o1_ref[...] + e2 * o2_ref[...]) * (1.0 / (e0 + e1 + e2))

    bra = jnp.dot(oa.astype(BF16), wa_ref[...], preferred_element_type=F32)
    brb = jnp.dot(ob_ref[...].astype(BF16), wb_ref[...], preferred_element_type=F32)
    merged = (gates[:, :d] * bra + gates[:, d:] * brb).astype(BF16)
    x1 = x + jnp.dot(merged, wo_ref[...], preferred_element_type=F32)
    x1_ref[...] = x1
    h2 = _rms(x1, gffn_ref[...])
    h2_hi = h2.astype(BF16)
    h2_ref[...] = h2_hi
    h2_lo = (h2 - h2_hi.astype(F32)).astype(BF16)

    nt = (((1,), (1,)), ((), ()))
    logits = (lax.dot_general(wrh_ref[...], h2_hi, nt, preferred_element_type=F32)
              + lax.dot_general(wrl_ref[...], h2_hi, nt, preferred_element_type=F32)
              + lax.dot_general(wrh_ref[...], h2_lo, nt, preferred_element_type=F32)
              + br_ref[...])
    eiota = lax.broadcasted_iota(I32, (N_EXPERTS, tm), 0)
    idxs, vals = [], []
    for _ in range(TOP_K):
        top = jnp.max(logits, axis=0, keepdims=True)
        idx = jnp.min(jnp.where(logits == top, eiota, N_EXPERTS), axis=0, keepdims=True)
        idxs.append(idx)
        vals.append(top)
        logits = jnp.where(eiota == idx, -jnp.inf, logits)
    ex = [jnp.exp(v - vals[0]) for v in vals]
    inv = 1.0 / (ex[0] + ex[1] + ex[2] + ex[3])
    e_ref[...] = jnp.concatenate(idxs, axis=0)
    gate_ref[...] = jnp.concatenate([v * inv for v in ex], axis=0)

    sub = pl.program_id(0) % sub_tiles

    @pl.when(sub == 0)
    def _():
        carry_ref[...] = jnp.zeros_like(carry_ref)

    onehots = [eiota == idx for idx in idxs]
    multi = onehots[0] | onehots[1] | onehots[2] | onehots[3]
    multi_f = jnp.where(multi, 1.0, 0.0)
    prefix = jnp.dot(multi_f.astype(BF16), tri_ref[...], preferred_element_type=F32)
    prefix = prefix + carry_ref[:, 0:1]
    q_ref[...] = jnp.concatenate(
        [jnp.sum(jnp.where(oh, prefix, 0.0), axis=0, keepdims=True) for oh in onehots],
        axis=0).astype(I32)
    carry = carry_ref[...] + jnp.sum(multi_f, axis=1, keepdims=True)
    carry_ref[...] = carry
    cnt_ref[...] = carry.astype(I32)


def _merge_router(x, o0, l0, o1, l1, o2, l2, ob, gmix, wg, wa, wb, wo, gffn, wrh, wrl, br):
    n, d = x.shape
    tm = TM_MERGE
    sub_tiles = TM_DISPATCH // tm
    tri = jnp.asarray(np.triu(np.ones((tm, tm), np.float32), k=1), BF16)

    def row(width):
        return pl.BlockSpec((tm, width), lambda i: (i, 0))

    def const(shape):
        return pl.BlockSpec(shape, lambda i: (0,) * len(shape))

    def tok4():
        return pl.BlockSpec((TOP_K, tm), lambda i: (0, i))

    return pl.pallas_call(
        functools.partial(_merge_kernel, sub_tiles=sub_tiles),
        grid=(n // tm,),
        in_specs=[row(d)] + [row(QUAD)] * 6 + [row(NA_QUADS * QUAD),
                  const((1, d)), const(wg.shape), const(wa.shape), const(wb.shape), const(wo.shape),
                  const((1, d)), const(wrh.shape), const(wrl.shape), const(br.shape), const(tri.shape)],
        out_specs=[row(d), row(d), tok4(), tok4(), tok4(),
                   pl.BlockSpec((None, N_EXPERTS, 128), lambda i: (i // sub_tiles, 0, 0))],
        out_shape=[jax.ShapeDtypeStruct((n, d), F32), jax.ShapeDtypeStruct((n, d), BF16),
                   jax.ShapeDtypeStruct((TOP_K, n), I32), jax.ShapeDtypeStruct((TOP_K, n), F32),
                   jax.ShapeDtypeStruct((TOP_K, n), I32),
                   jax.ShapeDtypeStruct((n // TM_DISPATCH, N_EXPERTS, 128), I32)],
        scratch_shapes=[pltpu.VMEM((N_EXPERTS, 128), F32)],
        compiler_params=_cparams("arbitrary"),
    )(x, o0, l0, o1, l1, o2, l2, ob, gmix, wg, wa, wb, wo, gffn, wrh, wrl, br, tri)


def _pack_bf16_pair(x):
    w = x.shape[1] // 2
    lo = pltpu.bitcast(x[:, :w], U32) >> 16
    hi = pltpu.bitcast(x[:, w:], U32) & jnp.uint32(0xFFFF0000)
    return lo | hi


def _unpack_bf16_pair(u):
    lo = pltpu.bitcast(u << 16, F32).astype(BF16)
    hi = pltpu.bitcast(u & jnp.uint32(0xFFFF0000), F32).astype(BF16)
    return lo, hi


def _segment_bits():
    return list(range((TM_DISPATCH // SEG_ALIGN).bit_length() - 1, -1, -1))


def _segment_copies(tile, c8_ref, off_ref, a_ref, make_copy, action):
    for e in range(N_EXPERTS):
        k = tile * N_EXPERTS + e
        units = c8_ref[k] // SEG_ALIGN
        stage0 = off_ref[k]
        hbm0 = a_ref[k]
        for b in _segment_bits():
            size = SEG_ALIGN << b

            @pl.when(((units >> b) & 1) == 1)
            def _():
                done = ((units >> (b + 1)) << (b + 1)) * SEG_ALIGN
                cp = make_copy(pl.multiple_of(stage0 + done, SEG_ALIGN),
                               pl.multiple_of(hbm0 + done, SEG_ALIGN), size)
                if action == "start":
                    cp.start()
                else:
                    cp.wait()


def _stage_positions(e, q, off_ref, tile):
    pos = q
    for ee in range(N_EXPERTS):
        pos = pos + jnp.where(e == ee, off_ref[tile * N_EXPERTS + ee], 0)
    return pos


def _dispatch_kernel(c8_ref, off_ref, a_ref, h2_ref, e_ref, q_ref, xs_in_ref, xs_ref, stage_ref, sem):
    del xs_in_ref
    tile = pl.program_id(0)
    s_rows, tm = stage_ref.shape[0], h2_ref.shape[0]
    pos = _stage_positions(e_ref[...], q_ref[...], off_ref, tile)
    siota = lax.broadcasted_iota(I32, (s_rows, tm), 0)
    hit = siota == pos[0:1, :]
    for k in range(1, TOP_K):
        hit = hit | (siota == pos[k:k + 1, :])
    compact = jnp.where(hit, 1.0, 0.0).astype(BF16)
    rows = jnp.dot(compact, h2_ref[...], preferred_element_type=F32)
    stage_ref[...] = _pack_bf16_pair(rows)

    def make_copy(stage_row, hbm_row, size):
        return pltpu.make_async_copy(stage_ref.at[pl.ds(stage_row, size)],
                                     xs_ref.at[pl.ds(hbm_row, size)], sem)

    _segment_copies(tile, c8_ref, off_ref, a_ref, make_copy, "start")
    _segment_copies(tile, c8_ref, off_ref, a_ref, make_copy, "wait")


def _dispatch(c8, off, a, h2, eidx, q, p_rows, s_rows):
    n, d = h2.shape
    tm = TM_DISPATCH
    xs0 = jnp.zeros((p_rows, d // 2), U32)
    grid_spec = pltpu.PrefetchScalarGridSpec(
        num_scalar_prefetch=3,
        grid=(n // tm,),
        in_specs=[pl.BlockSpec((tm, d), lambda i, *_: (i, 0)),
                  pl.BlockSpec((TOP_K, tm), lambda i, *_: (0, i)),
                  pl.BlockSpec((TOP_K, tm), lambda i, *_: (0, i)),
                  pl.BlockSpec(memory_space=pl.ANY)],
        out_specs=pl.BlockSpec(memory_space=pl.ANY),
        scratch_shapes=[pltpu.VMEM((s_rows, d // 2), U32), pltpu.SemaphoreType.DMA(())],
    )
    return pl.pallas_call(
        _dispatch_kernel,
        grid_spec=grid_spec,
        out_shape=jax.ShapeDtypeStruct((p_rows, d // 2), U32),
        input_output_aliases={6: 0},
        compiler_params=_cparams("arbitrary"),
    )(c8, off, a, h2, eidx, q, xs0)


def _expert_kernel(be_ref, nused_ref, xs_ref, wgu_ref, bgu_ref, wd_ref, bd_ref, ys_ref):
    del be_ref
    active = pl.program_id(0) < nused_ref[0]

    @pl.when(jnp.logical_not(active))
    def _():
        ys_ref[...] = jnp.zeros_like(ys_ref)

    @pl.when(active)
    def _():
        de = wd_ref.shape[0]
        lo, hi = _unpack_bf16_pair(xs_ref[...])
        x = jnp.concatenate([lo, hi], axis=1)
        gu = jnp.dot(x, wgu_ref[...], preferred_element_type=F32) + bgu_ref[...]
        x_glu = jnp.minimum(gu[:, :de], SWIGLU_LIMIT)
        x_lin = jnp.clip(gu[:, de:], -SWIGLU_LIMIT, SWIGLU_LIMIT)
        act = x_glu * jax.nn.sigmoid(SWIGLU_ALPHA * x_glu) * (x_lin + 1.0)
        y = jnp.dot(act.astype(BF16), wd_ref[...], preferred_element_type=F32) + bd_ref[...]
        ys_ref[...] = _pack_bf16_pair(y.astype(BF16).astype(F32))


def _expert_ffn(block_e, nused, xs, wgu, bgu, wd, bd):
    p_rows, half = xs.shape
    bm = EXPERT_BM
    d, de2 = wgu.shape[1], wgu.shape[2]

    def blk(b, be, nu):
        return (jnp.minimum(b, nu[0] - 1), 0)

    def wsel(b, be, nu):
        return (be[jnp.minimum(b, nu[0] - 1)], 0, 0)

    grid_spec = pltpu.PrefetchScalarGridSpec(
        num_scalar_prefetch=2,
        grid=(p_rows // bm,),
        in_specs=[pl.BlockSpec((bm, half), blk),
                  pl.BlockSpec((None, d, de2), wsel),
                  pl.BlockSpec((None, 1, de2), wsel),
                  pl.BlockSpec((None, de2 // 2, d), wsel),
                  pl.BlockSpec((None, 1, d), wsel)],
        out_specs=pl.BlockSpec((bm, half), lambda b, be, nu: (b, 0)),
    )
    return pl.pallas_call(
        _expert_kernel,
        grid_spec=grid_spec,
        out_shape=jax.ShapeDtypeStruct((p_rows, half), U32),
        compiler_params=_cparams("arbitrary"),
    )(block_e, nused, xs, wgu, bgu, wd, bd)


def _combine_kernel(c8_ref, off_ref, a_ref, ys_ref, x1_ref, e_ref, q_ref, gate_ref, g_ref, o_ref,
                    stage_ref, sem, *, tile0):
    step = pl.program_id(0)
    tile = step + tile0
    s_rows, tm = stage_ref.shape[0], x1_ref.shape[0]

    @pl.when(step == 0)
    def _():
        stage_ref[...] = jnp.zeros_like(stage_ref)

    def make_copy(stage_row, hbm_row, size):
        return pltpu.make_async_copy(ys_ref.at[pl.ds(hbm_row, size)],
                                     stage_ref.at[pl.ds(stage_row, size)], sem)

    _segment_copies(tile, c8_ref, off_ref, a_ref, make_copy, "start")
    pos = _stage_positions(e_ref[...], q_ref[...], off_ref, tile)
    gate = gate_ref[...]
    liota = lax.broadcasted_iota(I32, (tm, s_rows), 1)
    gmat = jnp.zeros((tm, s_rows), F32)
    for k in range(TOP_K):
        gmat = jnp.where(liota == pos[:, k:k + 1], gate[:, k:k + 1], gmat)
    gmat = gmat.astype(BF16)
    _segment_copies(tile, c8_ref, off_ref, a_ref, make_copy, "wait")
    lo, hi = _unpack_bf16_pair(stage_ref[...])
    moe = jnp.concatenate([jnp.dot(gmat, lo, preferred_element_type=F32),
                           jnp.dot(gmat, hi, preferred_element_type=F32)], axis=1)
    o_ref[...] = _rms(x1_ref[...] + moe, g_ref[...])


def _combine(c8, off, a, ys, x1, eidx_t, q_t, gate_t, gfinal, tok0, n_tok, s_rows):
    _, d = x1.shape
    tm = TM_DISPATCH
    tile0 = tok0 // tm

    def row(width):
        return pl.BlockSpec((tm, width), lambda i, *_: (i + tile0, 0))

    grid_spec = pltpu.PrefetchScalarGridSpec(
        num_scalar_prefetch=3,
        grid=(n_tok // tm,),
        in_specs=[pl.BlockSpec(memory_space=pl.ANY), row(d), row(TOP_K), row(TOP_K), row(TOP_K),
                  pl.BlockSpec((1, d), lambda i, *_: (0, 0))],
        out_specs=pl.BlockSpec((tm, d), lambda i, *_: (i, 0)),
        scratch_shapes=[pltpu.VMEM((s_rows, d // 2), U32), pltpu.SemaphoreType.DMA(())],
    )
    return pl.pallas_call(
        functools.partial(_combine_kernel, tile0=tile0),
        grid_spec=grid_spec,
        out_shape=jax.ShapeDtypeStruct((n_tok, d), F32),
        compiler_params=_cparams("arbitrary"),
    )(c8, off, a, ys, x1, eidx_t, q_t, gate_t, gfinal)


def _routing_tables(cnt, p_rows):
    c8 = _round_up(cnt, SEG_ALIGN)
    off = jnp.cumsum(c8, axis=1) - c8
    tot = jnp.sum(c8, axis=0)
    padded = _round_up(tot, EXPERT_BM)
    pend = jnp.cumsum(padded)
    a = (pend - padded)[None, :] + jnp.cumsum(c8, axis=0) - c8
    n_blocks = p_rows // EXPERT_BM
    block_e = jnp.searchsorted(pend, jnp.arange(n_blocks, dtype=I32) * EXPERT_BM, side="right")
    block_e = jnp.minimum(block_e, N_EXPERTS - 1).astype(I32)
    nused = (pend[-1] // EXPERT_BM).astype(I32).reshape(1)
    flat = lambda t: t.reshape(-1).astype(I32)
    return flat(c8), flat(off), flat(a), block_e, nused


def _permute_qkv_weight(w_in):
    d = w_in.shape[0]
    w = w_in.reshape(d, 3, N_QUADS, QUAD)
    scale = jnp.asarray([HEAD_DIM ** -0.5, 1.0, 1.0], F32).reshape(1, 3, 1, 1)
    return (w * scale).transpose(0, 2, 1, 3).reshape(d, N_QUADS * QKV_W).astype(BF16)


def kernel(x_prompt, x_sample, t5_rel_bias, norm_mix, w_in, na_rpb, w_branch_a, w_branch_b, w_gate, w_out,
           norm_ffn, w_router, b_router, w_gate_up, b_gate_up, w_down, b_down, norm_final):
    assert norm_mix.shape[0] == 1, "single-layer encoder"
    d = x_prompt.shape[-1]
    groups, tok0 = [], 0
    for xg in (x_prompt, x_sample):
        groups.append((xg.shape[0], xg.shape[1], tok0))
        tok0 += xg.shape[0] * xg.shape[1]
    n = tok0
    assert n % TM_QKV == 0 and n % TM_DISPATCH == 0 and all(g[2] % TM_DISPATCH == 0 for g in groups)
    x = jnp.concatenate([x_prompt.reshape(-1, d), x_sample.reshape(-1, d)], axis=0)

    qkv = _qkv_proj(x, norm_mix[0][None], _permute_qkv_weight(w_in[0]))

    band = []
    for g, (window, dilation) in enumerate(DILATED_GROUPS):
        assert window // (2 * dilation) == BAND_HALF
        bias = _band_bias(t5_rel_bias[:, g * QUAD_HEADS:(g + 1) * QUAD_HEADS], dilation, BAND_BQ)
        band.extend(_band_attention(qkv, g, dilation, bias, groups))
    ob = _neighbourhood_attention(qkv, _na_bias(na_rpb[0]), groups)

    wr = w_router[0].T
    wr_hi = wr.astype(BF16)
    wr_lo = (wr - wr_hi.astype(F32)).astype(BF16)
    x1, h2, eidx, gate, q, cnt = _merge_router(
        x, *band, ob, norm_mix[0][None], w_gate[0].astype(BF16), w_branch_a[0].astype(BF16),
        w_branch_b[0].astype(BF16), w_out[0].astype(BF16), norm_ffn[0][None], wr_hi, wr_lo,
        b_router[0][:, None])

    n_tiles = n // TM_DISPATCH
    p_rows = _round_up(n * TOP_K + n_tiles * N_EXPERTS * (SEG_ALIGN - 1) + N_EXPERTS * (EXPERT_BM - 1),
                       EXPERT_BM)
    s_rows = _round_up(TM_DISPATCH * TOP_K + N_EXPERTS * (SEG_ALIGN - 1), 256)
    c8, off, a, block_e, nused = _routing_tables(cnt[:, :, 0], p_rows)

    xs = _dispatch(c8, off, a, h2, eidx, q, p_rows, s_rows)
    ys = _expert_ffn(block_e, nused, xs, w_gate_up[0].astype(BF16), b_gate_up[0][:, None, :],
                     w_down[0].astype(BF16), b_down[0][:, None, :])

    eidx_t, q_t, gate_t = eidx.T, q.T, gate.T
    outs = []
    for (b, t, g0), xg in zip(groups, (x_prompt, x_sample)):
        y = _combine(c8, off, a, ys, x1, eidx_t, q_t, gate_t, norm_final[None], g0, b * t, s_rows)
        outs.append(y.reshape(xg.shape))
    return tuple(outs)
```

```python
import functools

import numpy as np
import jax
import jax.numpy as jnp
from jax import lax
from jax.experimental import pallas as pl
from jax.experimental.pallas import tpu as pltpu

F32 = jnp.float32
BF16 = jnp.bfloat16
I32 = jnp.int32
U32 = jnp.uint32

HEAD_DIM = 64
QUAD_HEADS = 4
QUAD = QUAD_HEADS * HEAD_DIM
LANES = 128
QKV_W = 3 * QUAD
DILATED_GROUPS = ((128, 1), (512, 4), (2048, 16))
N_DIL = len(DILATED_GROUPS)
NA_QUADS = 2
N_QUADS = N_DIL + NA_QUADS
BAND_HALF = 64
T5_BUCKETS = 32
T5_MAX_DISTANCE = 1024
GRID_W = 64
NA_KH = 8
NA_KW = 16
N_EXPERTS = 32
TOP_K = 4
SWIGLU_LIMIT = 7.0
SWIGLU_ALPHA = 1.702
RMS_EPS = 1e-6
NEG = -1e30

BAND_BQ = 128
BAND_OUTER = 512
NA_BAND_ROWS = 8
TM_QKV = 512
TM_DISPATCH = 512
TM_MERGE_SUB = 256
EXPERT_BM = 512
EXPERT_CHUNK = 256
SEG_ALIGN = 8
VMEM_LIMIT = 56 * 1024 * 1024


def _round_up(x, m):
    return (x + m - 1) // m * m


def _cparams(*sem):
    return pltpu.CompilerParams(dimension_semantics=sem, vmem_limit_bytes=VMEM_LIMIT)


def _rms(x, g):
    ms = jnp.mean(x * x, axis=-1, keepdims=True)
    return x * lax.rsqrt(ms + RMS_EPS) * g


def _select_group(w, metas, fn):
    out = fn(metas[0], w - metas[0]["base"])
    for m in metas[1:]:
        cand = fn(m, w - m["base"])
        sel = w >= m["base"]
        out = jax.tree.map(lambda a, b: jnp.where(sel, b, a), out, cand)
    return out


def _qkv_kernel(x_ref, g_ref, w_ref, *refs):
    band_refs, na_ref, h_ref = refs[:N_DIL], refs[N_DIL], refs[N_DIL + 1]
    n_lane_blocks, tm, _ = h_ref.shape
    h_nat = _rms(x_ref[...], g_ref[...])
    for c in range(n_lane_blocks):
        h_ref[c] = h_nat[:, c * LANES:(c + 1) * LANES]

    def project(h, j):
        return jnp.dot(h.astype(BF16), w_ref[:, j * QKV_W:(j + 1) * QKV_W],
                       preferred_element_type=F32).astype(BF16)

    def class_rows(r, rows, dil):
        return jnp.concatenate([h_ref[c, pl.ds(r, rows, stride=dil), :] for c in range(n_lane_blocks)],
                               axis=1)

    for j in range(NA_QUADS):
        na_ref[j] = project(h_nat, N_DIL + j)
    for g, (_, dil) in enumerate(DILATED_GROUPS):
        rows = tm // dil
        if dil == 1:
            band_refs[g][...] = project(h_nat, g)
            continue
        hp = jnp.concatenate([class_rows(r, rows, dil) for r in range(dil)], axis=0)
        y = project(hp, g)
        for r in range(dil):
            band_refs[g][:, r * QKV_W:(r + 1) * QKV_W] = y[r * rows:(r + 1) * rows]


def _qkv_proj(x, g, w):
    n, d = x.shape
    tm = TM_QKV
    band_specs = [pl.BlockSpec((tm // dil, dil * QKV_W), lambda i: (i, 0)) for _, dil in DILATED_GROUPS]
    band_shapes = [jax.ShapeDtypeStruct((n // dil, dil * QKV_W), BF16) for _, dil in DILATED_GROUPS]
    return pl.pallas_call(
        _qkv_kernel,
        grid=(n // tm,),
        in_specs=[pl.BlockSpec((tm, d), lambda i: (i, 0)),
                  pl.BlockSpec((1, d), lambda i: (0, 0)),
                  pl.BlockSpec((d, N_QUADS * QKV_W), lambda i: (0, 0))],
        out_specs=band_specs + [pl.BlockSpec((NA_QUADS, tm, QKV_W), lambda i: (0, i, 0))],
        out_shape=band_shapes + [jax.ShapeDtypeStruct((NA_QUADS, n, QKV_W), BF16)],
        scratch_shapes=[pltpu.VMEM((d // LANES, tm, LANES), F32)],
        compiler_params=_cparams("parallel"),
    )(x, g, w)


def _head_lane_masks(rows):
    lane_head = lax.broadcasted_iota(I32, (rows, QUAD), 1) // HEAD_DIM
    return [lane_head == h for h in range(QUAD_HEADS)]


def _stack_heads(q, masks):
    zero = jnp.zeros_like(q)
    return jnp.concatenate([jnp.where(m, q, zero) for m in masks], axis=0)


def _band_kernel(q_ref, kp_ref, kc_ref, kn_ref, vp_ref, vc_ref, vn_ref, b_ref, o_ref, l_ref,
                 *, decode, bq):
    n_sub = q_ref.shape[0] // bq
    t = decode(pl.program_id(0))
    first = (t["i"] == 0).astype(I32)
    last = (t["i"] == t["last"]).astype(I32)
    masks = _head_lane_masks(bq)
    k = jnp.concatenate([kp_ref[...], kc_ref[...], kn_ref[...]], axis=0)
    v = jnp.concatenate([vp_ref[...], vc_ref[...], vn_ref[...]], axis=0)
    for j in range(n_sub):
        rows = slice(j * bq, (j + 1) * bq)
        keys = slice(j * bq, (j + 1) * bq + 2 * BAND_HALF)
        variant = (first if j == 0 else 0) + (2 * last if j == n_sub - 1 else 0)
        qs = _stack_heads(q_ref[rows, :], masks)
        s = lax.dot_general(qs, k[keys], (((1,), (1,)), ((), ())), preferred_element_type=F32)
        s = s + b_ref[variant]
        m = jnp.max(s, axis=-1, keepdims=True)
        p = jnp.exp(s - m)
        l = jnp.sum(p, axis=-1, keepdims=True)
        pv = jnp.dot(p.astype(BF16), v[keys], preferred_element_type=F32) * (1.0 / l)
        lse = jnp.broadcast_to(m + jnp.log(l), pv.shape)
        o = jnp.zeros((bq, QUAD), F32)
        le = jnp.zeros((bq, QUAD), F32)
        for h in range(QUAD_HEADS):
            o = jnp.where(masks[h], pv[h * bq:(h + 1) * bq], o)
            le = jnp.where(masks[h], lse[h * bq:(h + 1) * bq], le)
        o_ref[rows, :] = o
        l_ref[rows, :] = le


def _t5_bucket(rel):
    nb = T5_BUCKETS // 2
    ret = (rel > 0).astype(np.int32) * nb
    n = np.abs(rel)
    max_exact = nb // 2
    large = max_exact + (np.log(np.maximum(n, 1) / max_exact)
                         / np.log(T5_MAX_DISTANCE / max_exact) * (nb - max_exact)).astype(np.int32)
    large = np.minimum(large, nb - 1)
    return ret + np.where(n < max_exact, n, large)


def _select_rows(table, index, n_rows):
    onehot = np.zeros((index.size, n_rows), np.float32)
    onehot[np.arange(index.size), index.reshape(-1)] = 1.0
    out = jnp.einsum("nb,b...->n...", jnp.asarray(onehot), table.astype(F32),
                     precision=lax.Precision.HIGHEST)
    return out.reshape(index.shape + table.shape[1:])


def _band_bias(table, dilation, bq):
    nk = bq + 2 * BAND_HALF
    qi = np.arange(bq)[:, None]
    c = np.arange(nk)[None, :]
    rel = c - BAND_HALF - qi
    band = np.abs(rel) <= BAND_HALF
    bias = jnp.transpose(_select_rows(table, _t5_bucket(dilation * rel), T5_BUCKETS), (2, 0, 1))
    out = []
    for variant in range(4):
        valid = band
        if variant & 1:
            valid = valid & (c >= BAND_HALF)
        if variant & 2:
            valid = valid & (c < bq + BAND_HALF)
        out.append(jnp.where(valid[None], bias, NEG).reshape(QUAD_HEADS * bq, nk))
    return jnp.stack(out, axis=0)


def _band_attention(a, dilation, bias, groups):
    d, bq = dilation, BAND_BQ
    n = a.shape[0] * d
    bo = min([BAND_OUTER] + [t // d for (_, t, _) in groups])
    hq = bo // BAND_HALF
    metas, base = [], 0
    for (b, t, tok0) in groups:
        seq = t // d
        assert bo % bq == 0 and seq % bo == 0 and tok0 % (d * bo) == 0
        nb = seq // bo
        metas.append(dict(base=base, nb=nb, rb0=tok0 // d // bo))
        base += b * d * nb
    n_items = base

    def decode(w):
        def f(m, wl):
            nb = m["nb"]
            s = wl // (d * nb)
            r = (wl // nb) % d
            i = wl % nb
            seq0 = m["rb0"] + s * nb
            return dict(r=r, i=i, seq0=seq0, last=nb - 1 + 0 * i)
        return _select_group(w, metas, f)

    def cur(col):
        def im(w):
            t = decode(w)
            return (t["seq0"] + t["i"], 3 * t["r"] + col)
        return pl.BlockSpec((bo, QUAD), im)

    def prev(col):
        def im(w):
            t = decode(w)
            return (hq * t["seq0"] + jnp.maximum(hq * t["i"] - 1, 0), 3 * t["r"] + col)
        return pl.BlockSpec((BAND_HALF, QUAD), im)

    def nxt(col):
        def im(w):
            t = decode(w)
            j = jnp.minimum(hq * t["i"] + hq, hq * t["last"] + hq - 1)
            return (hq * t["seq0"] + j, 3 * t["r"] + col)
        return pl.BlockSpec((BAND_HALF, QUAD), im)

    def out_im(w):
        t = decode(w)
        return (t["seq0"] + t["i"], t["r"])

    return pl.pallas_call(
        functools.partial(_band_kernel, decode=decode, bq=bq),
        grid=(n_items,),
        in_specs=[cur(0), prev(1), cur(1), nxt(1), prev(2), cur(2), nxt(2),
                  pl.BlockSpec(bias.shape, lambda w: (0, 0, 0))],
        out_specs=[pl.BlockSpec((bo, QUAD), out_im), pl.BlockSpec((bo, QUAD), out_im)],
        out_shape=[jax.ShapeDtypeStruct((n // d, d * QUAD), F32)] * 2,
        compiler_params=_cparams("parallel"),
    )(a, a, a, a, a, a, a, bias)


def _na_bias(rpb):
    qc = np.arange(GRID_W)[:, None]
    kc = np.arange(GRID_W)[None, :]
    col_idx = np.clip(kc - qc + NA_KW - 1, 0, 2 * NA_KW - 2)
    ws = np.clip(qc - NA_KW // 2, 0, GRID_W - NA_KW)
    mask = (kc >= ws) & (kc < ws + NA_KW)
    t = _select_rows(jnp.moveaxis(rpb, 2, 0), col_idx, 2 * NA_KW - 1)
    t = jnp.where(mask[:, :, None, None], t, NEG)
    b = jnp.stack([t[..., dv:dv + NA_KH] for dv in range(NA_KH)], axis=0)
    b = b.reshape(NA_KH, GRID_W, GRID_W, NA_QUADS, QUAD_HEADS, NA_KH)
    b = b.transpose(3, 0, 4, 1, 5, 2)
    return b.reshape(NA_QUADS, NA_KH, QUAD_HEADS * GRID_W, NA_KH * GRID_W)


def _na_kernel(q_ref, kp_ref, kc_ref, kn_ref, vp_ref, vc_ref, vn_ref, b_ref, o_ref,
               kbuf, vbuf, *, metas, items_per_quad):
    blk = NA_BAND_ROWS * GRID_W
    w = pl.program_id(0) % items_per_quad
    t = _select_group(w, metas, lambda m, wl: dict(i=wl % m["nbands"], nbands=m["nbands"] + 0 * wl))
    band, rows = t["i"], t["nbands"] * NA_BAND_ROWS
    kbuf[0:blk] = kp_ref[...]
    kbuf[blk:2 * blk] = kc_ref[...]
    kbuf[2 * blk:3 * blk] = kn_ref[...]
    vbuf[0:blk] = vp_ref[...]
    vbuf[blk:2 * blk] = vc_ref[...]
    vbuf[2 * blk:3 * blk] = vn_ref[...]
    masks = _head_lane_masks(GRID_W)

    for j in range(NA_BAND_ROWS):
        r = band * NA_BAND_ROWS + j
        rs = jnp.clip(r - NA_KH // 2, 0, rows - NA_KH)
        loc = pl.multiple_of((rs - (band - 1) * NA_BAND_ROWS) * GRID_W, GRID_W)
        kw = kbuf[pl.ds(loc, NA_KH * GRID_W), :]
        vw = vbuf[pl.ds(loc, NA_KH * GRID_W), :]
        qrow = slice(j * GRID_W, (j + 1) * GRID_W)
        qs = _stack_heads(q_ref[qrow, :], masks)
        s = lax.dot_general(qs, kw, (((1,), (1,)), ((), ())), preferred_element_type=F32)
        s = s + b_ref[rs - r + NA_KH - 1]
        m = jnp.max(s, axis=-1, keepdims=True)
        p = jnp.exp(s - m)
        l = jnp.sum(p, axis=-1, keepdims=True)
        pv = jnp.dot(p.astype(BF16), vw, preferred_element_type=F32) * (1.0 / l)
        o = jnp.zeros((GRID_W, QUAD), F32)
        for h in range(QUAD_HEADS):
            o = jnp.where(masks[h], pv[h * GRID_W:(h + 1) * GRID_W], o)
        o_ref[qrow, :] = o


def _neighbourhood_attention(qkv, bias, groups):
    _, n, _ = qkv.shape
    blk = NA_BAND_ROWS * GRID_W
    metas, base = [], 0
    for (b, t, tok0) in groups:
        rows = t // GRID_W
        assert t % GRID_W == 0 and rows % NA_BAND_ROWS == 0 and rows >= NA_KH and tok0 % blk == 0
        nbands = rows // NA_BAND_ROWS
        metas.append(dict(base=base, nbands=nbands, tb0=tok0 // blk))
        base += b * nbands
    items_per_quad = base

    def decode(w):
        quad = w // items_per_quad
        wq = w % items_per_quad

        def f(m, wl):
            s = wl // m["nbands"]
            i = wl % m["nbands"]
            return dict(i=i, seq0=m["tb0"] + s * m["nbands"], last=m["nbands"] - 1 + 0 * i)
        t = _select_group(wq, metas, f)
        t["quad"] = quad
        return t

    def spec(col, shift):
        def im(w):
            t = decode(w)
            i = jnp.clip(t["i"] + shift, 0, t["last"])
            return (t["quad"], t["seq0"] + i, col)
        return pl.BlockSpec((None, blk, QUAD), im)

    def out_im(w):
        t = decode(w)
        return (t["seq0"] + t["i"], t["quad"])

    return pl.pallas_call(
        functools.partial(_na_kernel, metas=metas, items_per_quad=items_per_quad),
        grid=(NA_QUADS * items_per_quad,),
        in_specs=[spec(0, 0), spec(1, -1), spec(1, 0), spec(1, 1), spec(2, -1), spec(2, 0), spec(2, 1),
                  pl.BlockSpec((None, NA_KH, QUAD_HEADS * GRID_W, NA_KH * GRID_W),
                               lambda w: (w // items_per_quad, 0, 0, 0))],
        out_specs=pl.BlockSpec((blk, QUAD), out_im),
        out_shape=jax.ShapeDtypeStruct((n, NA_QUADS * QUAD), F32),
        scratch_shapes=[pltpu.VMEM((3 * blk, QUAD), BF16), pltpu.VMEM((3 * blk, QUAD), BF16)],
        compiler_params=_cparams("parallel"),
    )(qkv, qkv, qkv, qkv, qkv, qkv, qkv, bias)


def _merge_kernel(x_ref, o0_ref, l0_ref, o1_ref, l1_ref, o2_ref, l2_ref, ob_ref,
                  gmix_ref, wg_ref, wa_ref, wb_ref, wo_ref, gffn_ref, wrh_ref, wrl_ref, br_ref, tri_ref,
                  x1_ref, h2_ref, e_ref, gate_ref, q_ref, cnt_ref, *nat_refs):
    d = x_ref.shape[1]
    tm = x_ref.shape[0]
    ts = tri_ref.shape[0]

    band = [(o0_ref, l0_ref)]
    pairs = ((o1_ref, l1_ref), (o2_ref, l2_ref))
    for g, (_, dil) in enumerate(DILATED_GROUPS[1:]):
        nat = nat_refs[2 * g:2 * g + 2]
        for src, dst in zip(pairs[g], nat):
            for r in range(dil):
                for c in range(QUAD // LANES):
                    col = r * QUAD + c * LANES
                    dst[c, pl.ds(r, tm // dil, stride=dil), :] = src[:, col:col + LANES]
        band.append(nat)

    eiota = lax.broadcasted_iota(I32, (N_EXPERTS, ts), 0)
    nt = (((1,), (1,)), ((), ()))
    carry = jnp.zeros((N_EXPERTS, 1), F32)
    for sub in range(tm // ts):
        rows = slice(sub * ts, (sub + 1) * ts)
        x = x_ref[rows, :]
        h = _rms(x, gmix_ref[...]).astype(BF16)
        gates = jax.nn.sigmoid(jnp.dot(h, wg_ref[...], preferred_element_type=F32))

        def rows_of(ref):
            if len(ref.shape) == 2:
                return ref[rows, :]
            return jnp.concatenate([ref[c, rows, :] for c in range(ref.shape[0])], axis=1)

        (o0, l0), (o1, l1), (o2, l2) = [(rows_of(o), rows_of(l)) for o, l in band]
        m = jnp.maximum(jnp.maximum(l0, l1), l2)
        e0, e1, e2 = jnp.exp(l0 - m), jnp.exp(l1 - m), jnp.exp(l2 - m)
        oa = (e0 * o0 + e1 * o1 + e2 * o2) * (1.0 / (e0 + e1 + e2))

        bra = jnp.dot(oa.astype(BF16), wa_ref[...], preferred_element_type=F32)
        brb = jnp.dot(ob_ref[rows, :].astype(BF16), wb_ref[...], preferred_element_type=F32)
        merged = (gates[:, :d] * bra + gates[:, d:] * brb).astype(BF16)
        x1 = x + jnp.dot(merged, wo_ref[...], preferred_element_type=F32)
        x1_ref[rows, :] = x1
        h2 = _rms(x1, gffn_ref[...])
        h2_hi = h2.astype(BF16)
        h2_ref[rows, :] = h2_hi
        h2_lo = (h2 - h2_hi.astype(F32)).astype(BF16)

        logits = (lax.dot_general(wrh_ref[...], h2_hi, nt, preferred_element_type=F32)
                  + lax.dot_general(wrl_ref[...], h2_hi, nt, preferred_element_type=F32)
                  + lax.dot_general(wrh_ref[...], h2_lo, nt, preferred_element_type=F32)
                  + br_ref[...])
        idxs, vals = [], []
        for _ in range(TOP_K):
            top = jnp.max(logits, axis=0, keepdims=True)
            idx = jnp.min(jnp.where(logits == top, eiota, N_EXPERTS), axis=0, keepdims=True)
            idxs.append(idx)
            vals.append(top)
            logits = jnp.where(eiota == idx, -jnp.inf, logits)
        ex = [jnp.exp(v - vals[0]) for v in vals]
        inv = 1.0 / (ex[0] + ex[1] + ex[2] + ex[3])
        e_ref[:, rows] = jnp.concatenate(idxs, axis=0)
        gate_ref[:, rows] = jnp.concatenate([v * inv for v in ex], axis=0)

        onehots = [eiota == idx for idx in idxs]
        multi = onehots[0] | onehots[1] | onehots[2] | onehots[3]
        multi_f = jnp.where(multi, 1.0, 0.0)
        prefix = jnp.dot(multi_f.astype(BF16), tri_ref[...], preferred_element_type=F32) + carry
        q_ref[:, rows] = jnp.concatenate(
            [jnp.sum(jnp.where(oh, prefix, 0.0), axis=0, keepdims=True) for oh in onehots],
            axis=0).astype(I32)
        carry = carry + jnp.sum(multi_f, axis=1, keepdims=True)
    cnt_ref[...] = jnp.broadcast_to(carry, cnt_ref.shape).astype(I32)


def _merge_router(x, band, ob, gmix, wg, wa, wb, wo, gffn, wrh, wrl, br):
    n, d = x.shape
    tm, ts = TM_DISPATCH, TM_MERGE_SUB
    tri = jnp.asarray(np.triu(np.ones((ts, ts), np.float32), k=1), BF16)

    def row(width, dil=1):
        return pl.BlockSpec((tm // dil, dil * width), lambda i: (i, 0))

    def const(shape):
        return pl.BlockSpec(shape, lambda i: (0,) * len(shape))

    def tok4():
        return pl.BlockSpec((TOP_K, tm), lambda i: (0, i))

    band_specs = [row(QUAD, dil) for _, dil in DILATED_GROUPS for _ in range(2)]
    band_args = [a for pair in band for a in pair]
    return pl.pallas_call(
        _merge_kernel,
        grid=(n // tm,),
        in_specs=[row(d)] + band_specs + [row(NA_QUADS * QUAD),
                  const((1, d)), const(wg.shape), const(wa.shape), const(wb.shape), const(wo.shape),
                  const((1, d)), const(wrh.shape), const(wrl.shape), const(br.shape), const(tri.shape)],
        out_specs=[row(d), row(d), tok4(), tok4(), tok4(),
                   pl.BlockSpec((None, N_EXPERTS, 128), lambda i: (i, 0, 0))],
        out_shape=[jax.ShapeDtypeStruct((n, d), F32), jax.ShapeDtypeStruct((n, d), BF16),
                   jax.ShapeDtypeStruct((TOP_K, n), I32), jax.ShapeDtypeStruct((TOP_K, n), F32),
                   jax.ShapeDtypeStruct((TOP_K, n), I32),
                   jax.ShapeDtypeStruct((n // tm, N_EXPERTS, 128), I32)],
        scratch_shapes=[pltpu.VMEM((QUAD // LANES, tm, LANES), F32)] * (2 * (N_DIL - 1)),
        compiler_params=_cparams("parallel"),
    )(x, *band_args, ob, gmix, wg, wa, wb, wo, gffn, wrh, wrl, br, tri)


def _pack_bf16_pair(x):
    w = x.shape[1] // 2
    lo = pltpu.bitcast(x[:, :w], U32) >> 16
    hi = pltpu.bitcast(x[:, w:], U32) & jnp.uint32(0xFFFF0000)
    return lo | hi


def _unpack_bf16_pair(u):
    lo = pltpu.bitcast(u << 16, F32).astype(BF16)
    hi = pltpu.bitcast(u & jnp.uint32(0xFFFF0000), F32).astype(BF16)
    return lo, hi


def _segment_bits():
    return list(range((TM_DISPATCH // SEG_ALIGN).bit_length() - 1, -1, -1))


def _segment_copies(tile, c8_ref, off_ref, a_ref, make_copy, action):
    for e in range(N_EXPERTS):
        k = tile * N_EXPERTS + e
        units = c8_ref[k] // SEG_ALIGN
        stage0 = off_ref[k]
        hbm0 = a_ref[k]
        for b in _segment_bits():
            size = SEG_ALIGN << b

            @pl.when(((units >> b) & 1) == 1)
            def _():
                done = ((units >> (b + 1)) << (b + 1)) * SEG_ALIGN
                cp = make_copy(pl.multiple_of(stage0 + done, SEG_ALIGN),
                               pl.multiple_of(hbm0 + done, SEG_ALIGN), size)
                if action == "start":
                    cp.start()
                else:
                    cp.wait()


def _stage_positions(e, q, off_ref, tile):
    pos = q
    for ee in range(N_EXPERTS):
        pos = pos + jnp.where(e == ee, off_ref[tile * N_EXPERTS + ee], 0)
    return pos


def _dispatch_kernel(c8_ref, off_ref, a_ref, h2_ref, e_ref, q_ref, xs_in_ref, xs_ref, stage_ref, sem):
    del xs_in_ref
    tile = pl.program_id(0)
    s_rows, tm = stage_ref.shape[0], h2_ref.shape[0]
    pos = _stage_positions(e_ref[...], q_ref[...], off_ref, tile)
    siota = lax.broadcasted_iota(I32, (s_rows, tm), 0)
    hit = siota == pos[0:1, :]
    for k in range(1, TOP_K):
        hit = hit | (siota == pos[k:k + 1, :])
    compact = jnp.where(hit, 1.0, 0.0).astype(BF16)
    rows = jnp.dot(compact, h2_ref[...], preferred_element_type=F32)
    stage_ref[...] = _pack_bf16_pair(rows)

    def make_copy(stage_row, hbm_row, size):
        return pltpu.make_async_copy(stage_ref.at[pl.ds(stage_row, size)],
                                     xs_ref.at[pl.ds(hbm_row, size)], sem)

    _segment_copies(tile, c8_ref, off_ref, a_ref, make_copy, "start")
    _segment_copies(tile, c8_ref, off_ref, a_ref, make_copy, "wait")


def _dispatch(c8, off, a, h2, eidx, q, p_rows, s_rows):
    n, d = h2.shape
    tm = TM_DISPATCH
    xs0 = jnp.zeros((p_rows, d // 2), U32)
    grid_spec = pltpu.PrefetchScalarGridSpec(
        num_scalar_prefetch=3,
        grid=(n // tm,),
        in_specs=[pl.BlockSpec((tm, d), lambda i, *_: (i, 0)),
                  pl.BlockSpec((TOP_K, tm), lambda i, *_: (0, i)),
                  pl.BlockSpec((TOP_K, tm), lambda i, *_: (0, i)),
                  pl.BlockSpec(memory_space=pl.ANY)],
        out_specs=pl.BlockSpec(memory_space=pl.ANY),
        scratch_shapes=[pltpu.VMEM((s_rows, d // 2), U32), pltpu.SemaphoreType.DMA(())],
    )
    return pl.pallas_call(
        _dispatch_kernel,
        grid_spec=grid_spec,
        out_shape=jax.ShapeDtypeStruct((p_rows, d // 2), U32),
        input_output_aliases={6: 0},
        compiler_params=_cparams("arbitrary"),
    )(c8, off, a, h2, eidx, q, xs0)


def _expert_kernel(be_ref, nused_ref, xs_ref, wgu_ref, bgu_ref, wd_ref, bd_ref, ys_ref):
    del be_ref
    active = pl.program_id(0) < nused_ref[0]

    @pl.when(jnp.logical_not(active))
    def _():
        ys_ref[...] = jnp.zeros_like(ys_ref)

    @pl.when(active)
    def _():
        lo, hi = _unpack_bf16_pair(xs_ref[...])
        x = jnp.concatenate([lo, hi], axis=1)
        ck = EXPERT_CHUNK
        y = None
        for c in range(wd_ref.shape[0] // ck):
            cols = slice(2 * c * ck, 2 * (c + 1) * ck)
            gu = jnp.dot(x, wgu_ref[:, cols], preferred_element_type=F32) + bgu_ref[:, cols]
            x_glu = jnp.minimum(gu[:, :ck], SWIGLU_LIMIT)
            x_lin = jnp.clip(gu[:, ck:], -SWIGLU_LIMIT, SWIGLU_LIMIT)
            act = x_glu * jax.nn.sigmoid(SWIGLU_ALPHA * x_glu) * (x_lin + 1.0)
            part = jnp.dot(act.astype(BF16), wd_ref[c * ck:(c + 1) * ck, :], preferred_element_type=F32)
            y = part if y is None else y + part
        y = y + bd_ref[...]
        ys_ref[...] = _pack_bf16_pair(y.astype(BF16).astype(F32))


def _expert_ffn(block_e, nused, xs, wgu, bgu, wd, bd):
    p_rows, half = xs.shape
    bm = EXPERT_BM
    d, de2 = wgu.shape[1], wgu.shape[2]

    def blk(b, be, nu):
        return (jnp.minimum(b, nu[0] - 1), 0)

    def wsel(b, be, nu):
        return (be[jnp.minimum(b, nu[0] - 1)], 0, 0)

    grid_spec = pltpu.PrefetchScalarGridSpec(
        num_scalar_prefetch=2,
        grid=(p_rows // bm,),
        in_specs=[pl.BlockSpec((bm, half), blk),
                  pl.BlockSpec((None, d, de2), wsel),
                  pl.BlockSpec((None, 1, de2), wsel),
                  pl.BlockSpec((None, de2 // 2, d), wsel),
                  pl.BlockSpec((None, 1, d), wsel)],
        out_specs=pl.BlockSpec((bm, half), lambda b, be, nu: (b, 0)),
    )
    return pl.pallas_call(
        _expert_kernel,
        grid_spec=grid_spec,
        out_shape=jax.ShapeDtypeStruct((p_rows, half), U32),
        compiler_params=_cparams("arbitrary"),
    )(block_e, nused, xs, wgu, bgu, wd, bd)


def _combine_kernel(c8_ref, off_ref, a_ref, ys_ref, x1_ref, e_ref, q_ref, gate_ref, g_ref, o_ref,
                    stage_ref, sem, *, tile0):
    step = pl.program_id(0)
    tile = step + tile0
    s_rows, tm = stage_ref.shape[0], x1_ref.shape[0]

    @pl.when(step == 0)
    def _():
        stage_ref[...] = jnp.zeros_like(stage_ref)

    def make_copy(stage_row, hbm_row, size):
        return pltpu.make_async_copy(ys_ref.at[pl.ds(hbm_row, size)],
                                     stage_ref.at[pl.ds(stage_row, size)], sem)

    _segment_copies(tile, c8_ref, off_ref, a_ref, make_copy, "start")
    pos = _stage_positions(e_ref[...], q_ref[...], off_ref, tile)
    gate = gate_ref[...]
    liota = lax.broadcasted_iota(I32, (tm, s_rows), 1)
    gmat = jnp.zeros((tm, s_rows), F32)
    for k in range(TOP_K):
        gmat = jnp.where(liota == pos[:, k:k + 1], gate[:, k:k + 1], gmat)
    gmat = gmat.astype(BF16)
    _segment_copies(tile, c8_ref, off_ref, a_ref, make_copy, "wait")
    lo, hi = _unpack_bf16_pair(stage_ref[...])
    moe = jnp.concatenate([jnp.dot(gmat, lo, preferred_element_type=F32),
                           jnp.dot(gmat, hi, preferred_element_type=F32)], axis=1)
    o_ref[...] = _rms(x1_ref[...] + moe, g_ref[...])


def _combine(c8, off, a, ys, x1, eidx_t, q_t, gate_t, gfinal, tok0, n_tok, s_rows):
    _, d = x1.shape
    tm = TM_DISPATCH
    tile0 = tok0 // tm

    def row(width):
        return pl.BlockSpec((tm, width), lambda i, *_: (i + tile0, 0))

    grid_spec = pltpu.PrefetchScalarGridSpec(
        num_scalar_prefetch=3,
        grid=(n_tok // tm,),
        in_specs=[pl.BlockSpec(memory_space=pl.ANY), row(d), row(TOP_K), row(TOP_K), row(TOP_K),
                  pl.BlockSpec((1, d), lambda i, *_: (0, 0))],
        out_specs=pl.BlockSpec((tm, d), lambda i, *_: (i, 0)),
        scratch_shapes=[pltpu.VMEM((s_rows, d // 2), U32), pltpu.SemaphoreType.DMA(())],
    )
    return pl.pallas_call(
        functools.partial(_combine_kernel, tile0=tile0),
        grid_spec=grid_spec,
        out_shape=jax.ShapeDtypeStruct((n_tok, d), F32),
        compiler_params=_cparams("arbitrary"),
    )(c8, off, a, ys, x1, eidx_t, q_t, gate_t, gfinal)


def _routing_tables(cnt, p_rows):
    c8 = _round_up(cnt, SEG_ALIGN)
    off = jnp.cumsum(c8, axis=1) - c8
    tot = jnp.sum(c8, axis=0)
    padded = _round_up(tot, EXPERT_BM)
    pend = jnp.cumsum(padded)
    a = (pend - padded)[None, :] + jnp.cumsum(c8, axis=0) - c8
    n_blocks = p_rows // EXPERT_BM
    block_row = jnp.arange(n_blocks, dtype=I32) * EXPERT_BM
    block_e = jnp.sum((block_row[:, None] >= pend[None, :]).astype(I32), axis=1)
    block_e = jnp.minimum(block_e, N_EXPERTS - 1).astype(I32)
    nused = (pend[-1] // EXPERT_BM).astype(I32).reshape(1)
    flat = lambda t: t.reshape(-1).astype(I32)
    return flat(c8), flat(off), flat(a), block_e, nused


def _permute_qkv_weight(w_in):
    d = w_in.shape[0]
    w = w_in.reshape(d, 3, N_QUADS, QUAD)
    scale = jnp.asarray([HEAD_DIM ** -0.5, 1.0, 1.0], F32).reshape(1, 3, 1, 1)
    return (w * scale).transpose(0, 2, 1, 3).reshape(d, N_QUADS * QKV_W).astype(BF16)


def _chunk_gate_up(w):
    e, k, h2 = w.shape
    nc = h2 // 2 // EXPERT_CHUNK
    return w.reshape(e, k, 2, nc, EXPERT_CHUNK).transpose(0, 1, 3, 2, 4).reshape(e, k, h2)


def kernel(x_prompt, x_sample, t5_rel_bias, norm_mix, w_in, na_rpb, w_branch_a, w_branch_b, w_gate, w_out,
           norm_ffn, w_router, b_router, w_gate_up, b_gate_up, w_down, b_down, norm_final):
    assert norm_mix.shape[0] == 1, "single-layer encoder"
    d = x_prompt.shape[-1]
    groups, tok0 = [], 0
    for xg in (x_prompt, x_sample):
        groups.append((xg.shape[0], xg.shape[1], tok0))
        tok0 += xg.shape[0] * xg.shape[1]
    n = tok0
    assert n % TM_QKV == 0 and n % TM_DISPATCH == 0 and all(g[2] % TM_DISPATCH == 0 for g in groups)
    x = jnp.concatenate([x_prompt.reshape(-1, d), x_sample.reshape(-1, d)], axis=0)

    *qkv_band, qkv_na = _qkv_proj(x, norm_mix[0][None], _permute_qkv_weight(w_in[0]))

    band = []
    for g, (window, dilation) in enumerate(DILATED_GROUPS):
        assert window // (2 * dilation) == BAND_HALF
        bias = _band_bias(t5_rel_bias[:, g * QUAD_HEADS:(g + 1) * QUAD_HEADS], dilation, BAND_BQ)
        band.append(_band_attention(qkv_band[g], dilation, bias, groups))
    ob = _neighbourhood_attention(qkv_na, _na_bias(na_rpb[0]), groups)

    wr = w_router[0].T
    wr_hi = wr.astype(BF16)
    wr_lo = (wr - wr_hi.astype(F32)).astype(BF16)
    x1, h2, eidx, gate, q, cnt = _merge_router(
        x, band, ob, norm_mix[0][None], w_gate[0].astype(BF16), w_branch_a[0].astype(BF16),
        w_branch_b[0].astype(BF16), w_out[0].astype(BF16), norm_ffn[0][None], wr_hi, wr_lo,
        b_router[0][:, None])

    n_tiles = n // TM_DISPATCH
    p_rows = _round_up(n * TOP_K + n_tiles * N_EXPERTS * (SEG_ALIGN - 1) + N_EXPERTS * (EXPERT_BM - 1),
                       EXPERT_BM)
    s_rows = _round_up(TM_DISPATCH * TOP_K + N_EXPERTS * (SEG_ALIGN - 1), 256)
    c8, off, a, block_e, nused = _routing_tables(cnt[:, :, 0], p_rows)

    xs = _dispatch(c8, off, a, h2, eidx, q, p_rows, s_rows)
    ys = _expert_ffn(block_e, nused, xs, _chunk_gate_up(w_gate_up[0]).astype(BF16),
                     _chunk_gate_up(b_gate_up[0][:, None, :]), w_down[0].astype(BF16), b_down[0][:, None, :])

    eidx_t, q_t, gate_t = eidx.T, q.T, gate.T
    outs = []
    for (b, t, g0), xg in zip(groups, (x_prompt, x_sample)):
        y = _combine(c8, off, a, ys, x1, eidx_t, q_t, gate_t, norm_final[None], g0, b * t, s_rows)
        outs.append(y.reshape(xg.shape))
    return tuple(outs)
```

```python
import functools

import numpy as np
import jax
import jax.numpy as jnp
from jax import lax
from jax.experimental import pallas as pl
from jax.experimental.pallas import tpu as pltpu

F32 = jnp.float32
BF16 = jnp.bfloat16
I32 = jnp.int32
U32 = jnp.uint32

HEAD_DIM = 64
QUAD_HEADS = 4
QUAD = QUAD_HEADS * HEAD_DIM
LANES = 128
QKV_W = 3 * QUAD
DILATED_GROUPS = ((128, 1), (512, 4), (2048, 16))
N_DIL = len(DILATED_GROUPS)
NA_QUADS = 2
N_QUADS = N_DIL + NA_QUADS
BAND_HALF = 64
T5_BUCKETS = 32
T5_MAX_DISTANCE = 1024
GRID_W = 64
NA_KH = 8
NA_KW = 16
N_EXPERTS = 32
TOP_K = 4
SWIGLU_LIMIT = 7.0
SWIGLU_ALPHA = 1.702
RMS_EPS = 1e-6
NEG = -1e30

BAND_BQ = 128
BAND_OUTER = 512
NA_BAND_ROWS = 8
TM_QKV = 512
TM_DISPATCH = 512
TM_MERGE_SUB = 256
EXPERT_BM = 512
EXPERT_CHUNK = 256
ONEHOT_CHUNK = 256
SEG_ALIGN = 8
VMEM_LIMIT = 56 * 1024 * 1024


def _round_up(x, m):
    return (x + m - 1) // m * m


def _cparams(*sem):
    return pltpu.CompilerParams(dimension_semantics=sem, vmem_limit_bytes=VMEM_LIMIT)


def _rms(x, g):
    ms = jnp.mean(x * x, axis=-1, keepdims=True)
    return x * lax.rsqrt(ms + RMS_EPS) * g


def _select_group(w, metas, fn):
    out = fn(metas[0], w - metas[0]["base"])
    for m in metas[1:]:
        cand = fn(m, w - m["base"])
        sel = w >= m["base"]
        out = jax.tree.map(lambda a, b: jnp.where(sel, b, a), out, cand)
    return out


def _group_tile_specs(xs, tm):
    specs, bounds, tile0 = [], [], 0
    for xg in xs:
        b, t, d = xg.shape
        assert t % tm == 0
        per_seq, n_tiles = t // tm, b * t // tm

        def im(i, *_, tile0=tile0, per_seq=per_seq, n_tiles=n_tiles):
            j = jnp.clip(i - tile0, 0, n_tiles - 1)
            return (j // per_seq, j % per_seq, 0)
        specs.append(pl.BlockSpec((None, tm, d), im))
        tile0 += n_tiles
        bounds.append(tile0)
    return specs, bounds


def _group_tile(refs, bounds, rows=slice(None)):
    i = pl.program_id(0)
    x = refs[-1][rows, :]
    for ref, bound in zip(refs[-2::-1], bounds[-2::-1]):
        x = jnp.where(i < bound, ref[rows, :], x)
    return x


def _qkv_kernel(*refs, bounds):
    ng = len(bounds)
    x_refs, (g_ref, w_ref), refs = refs[:ng], refs[ng:ng + 2], refs[ng + 2:]
    band_refs, na_ref, h_ref = refs[:N_DIL], refs[N_DIL], refs[N_DIL + 1]
    n_lane_blocks, tm, _ = h_ref.shape
    h_nat = _rms(_group_tile(x_refs, bounds), g_ref[...])
    for c in range(n_lane_blocks):
        h_ref[c] = h_nat[:, c * LANES:(c + 1) * LANES]

    def project(h, j):
        return jnp.dot(h.astype(BF16), w_ref[:, j * QKV_W:(j + 1) * QKV_W],
                       preferred_element_type=F32).astype(BF16)

    def class_rows(r, rows, dil):
        return jnp.concatenate([h_ref[c, pl.ds(r, rows, stride=dil), :] for c in range(n_lane_blocks)],
                               axis=1)

    for j in range(NA_QUADS):
        na_ref[j] = project(h_nat, N_DIL + j)
    for g, (_, dil) in enumerate(DILATED_GROUPS):
        rows = tm // dil
        if dil == 1:
            band_refs[g][...] = project(h_nat, g)
            continue
        hp = jnp.concatenate([class_rows(r, rows, dil) for r in range(dil)], axis=0)
        y = project(hp, g)
        for r in range(dil):
            band_refs[g][:, r * QKV_W:(r + 1) * QKV_W] = y[r * rows:(r + 1) * rows]


def _qkv_proj(xs, g, w):
    d = xs[0].shape[-1]
    tm = TM_QKV
    x_specs, bounds = _group_tile_specs(xs, tm)
    n = bounds[-1] * tm
    band_specs = [pl.BlockSpec((tm // dil, dil * QKV_W), lambda i: (i, 0)) for _, dil in DILATED_GROUPS]
    band_shapes = [jax.ShapeDtypeStruct((n // dil, dil * QKV_W), BF16) for _, dil in DILATED_GROUPS]
    return pl.pallas_call(
        functools.partial(_qkv_kernel, bounds=bounds),
        grid=(n // tm,),
        in_specs=x_specs + [pl.BlockSpec((1, d), lambda i: (0, 0)),
                            pl.BlockSpec((d, N_QUADS * QKV_W), lambda i: (0, 0))],
        out_specs=band_specs + [pl.BlockSpec((NA_QUADS, tm, QKV_W), lambda i: (0, i, 0))],
        out_shape=band_shapes + [jax.ShapeDtypeStruct((NA_QUADS, n, QKV_W), BF16)],
        scratch_shapes=[pltpu.VMEM((d // LANES, tm, LANES), F32)],
        compiler_params=_cparams("parallel"),
    )(*xs, g, w)


def _head_lane_masks(rows):
    lane_head = lax.broadcasted_iota(I32, (rows, QUAD), 1) // HEAD_DIM
    return [lane_head == h for h in range(QUAD_HEADS)]


def _stack_heads(q, masks):
    zero = jnp.zeros_like(q)
    return jnp.concatenate([jnp.where(m, q, zero) for m in masks], axis=0)


def _band_kernel(q_ref, kp_ref, kc_ref, kn_ref, vp_ref, vc_ref, vn_ref, b_ref, o_ref, l_ref,
                 *, decode, bq):
    n_sub = q_ref.shape[0] // bq
    t = decode(pl.program_id(0))
    first = (t["i"] == 0).astype(I32)
    last = (t["i"] == t["last"]).astype(I32)
    masks = _head_lane_masks(bq)
    k = jnp.concatenate([kp_ref[...], kc_ref[...], kn_ref[...]], axis=0)
    v = jnp.concatenate([vp_ref[...], vc_ref[...], vn_ref[...]], axis=0)
    for j in range(n_sub):
        rows = slice(j * bq, (j + 1) * bq)
        keys = slice(j * bq, (j + 1) * bq + 2 * BAND_HALF)
        variant = (first if j == 0 else 0) + (2 * last if j == n_sub - 1 else 0)
        qs = _stack_heads(q_ref[rows, :], masks)
        s = lax.dot_general(qs, k[keys], (((1,), (1,)), ((), ())), preferred_element_type=F32)
        s = s + b_ref[variant]
        m = jnp.max(s, axis=-1, keepdims=True)
        p = jnp.exp(s - m)
        l = jnp.sum(p, axis=-1, keepdims=True)
        pv = jnp.dot(p.astype(BF16), v[keys], preferred_element_type=F32) * (1.0 / l)
        lse = jnp.broadcast_to(m + jnp.log(l), pv.shape)
        o = jnp.zeros((bq, QUAD), F32)
        le = jnp.zeros((bq, QUAD), F32)
        for h in range(QUAD_HEADS):
            o = jnp.where(masks[h], pv[h * bq:(h + 1) * bq], o)
            le = jnp.where(masks[h], lse[h * bq:(h + 1) * bq], le)
        o_ref[rows, :] = o
        l_ref[rows, :] = le


def _t5_bucket(rel):
    nb = T5_BUCKETS // 2
    ret = (rel > 0).astype(np.int32) * nb
    n = np.abs(rel)
    max_exact = nb // 2
    large = max_exact + (np.log(np.maximum(n, 1) / max_exact)
                         / np.log(T5_MAX_DISTANCE / max_exact) * (nb - max_exact)).astype(np.int32)
    large = np.minimum(large, nb - 1)
    return ret + np.where(n < max_exact, n, large)


def _select_rows(table, index, n_rows):
    onehot = np.zeros((index.size, n_rows), np.float32)
    onehot[np.arange(index.size), index.reshape(-1)] = 1.0
    out = jnp.einsum("nb,b...->n...", jnp.asarray(onehot), table.astype(F32),
                     precision=lax.Precision.HIGHEST)
    return out.reshape(index.shape + table.shape[1:])


def _band_bias(table, dilation, bq):
    nk = bq + 2 * BAND_HALF
    qi = np.arange(bq)[:, None]
    c = np.arange(nk)[None, :]
    rel = c - BAND_HALF - qi
    band = np.abs(rel) <= BAND_HALF
    bias = jnp.transpose(_select_rows(table, _t5_bucket(dilation * rel), T5_BUCKETS), (2, 0, 1))
    out = []
    for variant in range(4):
        valid = band
        if variant & 1:
            valid = valid & (c >= BAND_HALF)
        if variant & 2:
            valid = valid & (c < bq + BAND_HALF)
        out.append(jnp.where(valid[None], bias, NEG).reshape(QUAD_HEADS * bq, nk))
    return jnp.stack(out, axis=0)


def _band_attention(a, dilation, bias, groups):
    d, bq = dilation, BAND_BQ
    n = a.shape[0] * d
    bo = min([BAND_OUTER] + [t // d for (_, t, _) in groups])
    hq = bo // BAND_HALF
    metas, base = [], 0
    for (b, t, tok0) in groups:
        seq = t // d
        assert bo % bq == 0 and seq % bo == 0 and tok0 % (d * bo) == 0
        nb = seq // bo
        metas.append(dict(base=base, nb=nb, rb0=tok0 // d // bo))
        base += b * d * nb
    n_items = base

    def decode(w):
        def f(m, wl):
            nb = m["nb"]
            s = wl // (d * nb)
            r = (wl // nb) % d
            i = wl % nb
            seq0 = m["rb0"] + s * nb
            return dict(r=r, i=i, seq0=seq0, last=nb - 1 + 0 * i)
        return _select_group(w, metas, f)

    def cur(col):
        def im(w):
            t = decode(w)
            return (t["seq0"] + t["i"], 3 * t["r"] + col)
        return pl.BlockSpec((bo, QUAD), im)

    def prev(col):
        def im(w):
            t = decode(w)
            return (hq * t["seq0"] + jnp.maximum(hq * t["i"] - 1, 0), 3 * t["r"] + col)
        return pl.BlockSpec((BAND_HALF, QUAD), im)

    def nxt(col):
        def im(w):
            t = decode(w)
            j = jnp.minimum(hq * t["i"] + hq, hq * t["last"] + hq - 1)
            return (hq * t["seq0"] + j, 3 * t["r"] + col)
        return pl.BlockSpec((BAND_HALF, QUAD), im)

    def out_im(w):
        t = decode(w)
        return (t["seq0"] + t["i"], t["r"])

    return pl.pallas_call(
        functools.partial(_band_kernel, decode=decode, bq=bq),
        grid=(n_items,),
        in_specs=[cur(0), prev(1), cur(1), nxt(1), prev(2), cur(2), nxt(2),
                  pl.BlockSpec(bias.shape, lambda w: (0, 0, 0))],
        out_specs=[pl.BlockSpec((bo, QUAD), out_im), pl.BlockSpec((bo, QUAD), out_im)],
        out_shape=[jax.ShapeDtypeStruct((n // d, d * QUAD), F32)] * 2,
        compiler_params=_cparams("parallel"),
    )(a, a, a, a, a, a, a, bias)


def _na_bias(rpb):
    qc = np.arange(GRID_W)[:, None]
    kc = np.arange(GRID_W)[None, :]
    col_idx = np.clip(kc - qc + NA_KW - 1, 0, 2 * NA_KW - 2)
    ws = np.clip(qc - NA_KW // 2, 0, GRID_W - NA_KW)
    mask = (kc >= ws) & (kc < ws + NA_KW)
    t = _select_rows(jnp.moveaxis(rpb, 2, 0), col_idx, 2 * NA_KW - 1)
    t = jnp.where(mask[:, :, None, None], t, NEG)
    b = jnp.stack([t[..., dv:dv + NA_KH] for dv in range(NA_KH)], axis=0)
    b = b.reshape(NA_KH, GRID_W, GRID_W, NA_QUADS, QUAD_HEADS, NA_KH)
    b = b.transpose(3, 0, 4, 1, 5, 2)
    return b.reshape(NA_QUADS, NA_KH, QUAD_HEADS * GRID_W, NA_KH * GRID_W)


def _na_kernel(q_ref, kp_ref, kc_ref, kn_ref, vp_ref, vc_ref, vn_ref, b_ref, o_ref,
               kbuf, vbuf, *, metas, items_per_quad):
    blk = NA_BAND_ROWS * GRID_W
    w = pl.program_id(0) % items_per_quad
    t = _select_group(w, metas, lambda m, wl: dict(i=wl % m["nbands"], nbands=m["nbands"] + 0 * wl))
    band, rows = t["i"], t["nbands"] * NA_BAND_ROWS
    kbuf[0:blk] = kp_ref[...]
    kbuf[blk:2 * blk] = kc_ref[...]
    kbuf[2 * blk:3 * blk] = kn_ref[...]
    vbuf[0:blk] = vp_ref[...]
    vbuf[blk:2 * blk] = vc_ref[...]
    vbuf[2 * blk:3 * blk] = vn_ref[...]
    masks = _head_lane_masks(GRID_W)

    for j in range(NA_BAND_ROWS):
        r = band * NA_BAND_ROWS + j
        rs = jnp.clip(r - NA_KH // 2, 0, rows - NA_KH)
        loc = pl.multiple_of((rs - (band - 1) * NA_BAND_ROWS) * GRID_W, GRID_W)
        kw = kbuf[pl.ds(loc, NA_KH * GRID_W), :]
        vw = vbuf[pl.ds(loc, NA_KH * GRID_W), :]
        qrow = slice(j * GRID_W, (j + 1) * GRID_W)
        qs = _stack_heads(q_ref[qrow, :], masks)
        s = lax.dot_general(qs, kw, (((1,), (1,)), ((), ())), preferred_element_type=F32)
        s = s + b_ref[rs - r + NA_KH - 1]
        m = jnp.max(s, axis=-1, keepdims=True)
        p = jnp.exp(s - m)
        l = jnp.sum(p, axis=-1, keepdims=True)
        pv = jnp.dot(p.astype(BF16), vw, preferred_element_type=F32) * (1.0 / l)
        o = jnp.zeros((GRID_W, QUAD), F32)
        for h in range(QUAD_HEADS):
            o = jnp.where(masks[h], pv[h * GRID_W:(h + 1) * GRID_W], o)
        o_ref[qrow, :] = o


def _neighbourhood_attention(qkv, bias, groups):
    _, n, _ = qkv.shape
    blk = NA_BAND_ROWS * GRID_W
    metas, base = [], 0
    for (b, t, tok0) in groups:
        rows = t // GRID_W
        assert t % GRID_W == 0 and rows % NA_BAND_ROWS == 0 and rows >= NA_KH and tok0 % blk == 0
        nbands = rows // NA_BAND_ROWS
        metas.append(dict(base=base, nbands=nbands, tb0=tok0 // blk))
        base += b * nbands
    items_per_quad = base

    def decode(w):
        quad = w // items_per_quad
        wq = w % items_per_quad

        def f(m, wl):
            s = wl // m["nbands"]
            i = wl % m["nbands"]
            return dict(i=i, seq0=m["tb0"] + s * m["nbands"], last=m["nbands"] - 1 + 0 * i)
        t = _select_group(wq, metas, f)
        t["quad"] = quad
        return t

    def spec(col, shift):
        def im(w):
            t = decode(w)
            i = jnp.clip(t["i"] + shift, 0, t["last"])
            return (t["quad"], t["seq0"] + i, col)
        return pl.BlockSpec((None, blk, QUAD), im)

    def out_im(w):
        t = decode(w)
        return (t["seq0"] + t["i"], t["quad"])

    return pl.pallas_call(
        functools.partial(_na_kernel, metas=metas, items_per_quad=items_per_quad),
        grid=(NA_QUADS * items_per_quad,),
        in_specs=[spec(0, 0), spec(1, -1), spec(1, 0), spec(1, 1), spec(2, -1), spec(2, 0), spec(2, 1),
                  pl.BlockSpec((None, NA_KH, QUAD_HEADS * GRID_W, NA_KH * GRID_W),
                               lambda w: (w // items_per_quad, 0, 0, 0))],
        out_specs=pl.BlockSpec((blk, QUAD), out_im),
        out_shape=jax.ShapeDtypeStruct((n, NA_QUADS * QUAD), F32),
        scratch_shapes=[pltpu.VMEM((3 * blk, QUAD), BF16), pltpu.VMEM((3 * blk, QUAD), BF16)],
        compiler_params=_cparams("parallel"),
    )(qkv, qkv, qkv, qkv, qkv, qkv, qkv, bias)


def _merge_kernel(*refs, bounds):
    ng = len(bounds)
    x_refs, refs = refs[:ng], refs[ng:]
    (o0_ref, l0_ref, o1_ref, l1_ref, o2_ref, l2_ref, ob_ref,
     gmix_ref, wg_ref, wa_ref, wb_ref, wo_ref, gffn_ref, wr_ref, br_ref, tri_ref,
     x1_ref, h2_ref, gate_ref, pos_ref, cnt_ref, h2lo_ref, *nat_refs) = refs
    tm, d = x1_ref.shape
    ts = TM_MERGE_SUB

    band = [(o0_ref, l0_ref)]
    pairs = ((o1_ref, l1_ref), (o2_ref, l2_ref))
    for g, (_, dil) in enumerate(DILATED_GROUPS[1:]):
        nat = nat_refs[2 * g:2 * g + 2]
        for src, dst in zip(pairs[g], nat):
            for r in range(dil):
                for c in range(QUAD // LANES):
                    col = r * QUAD + c * LANES
                    dst[c, pl.ds(r, tm // dil, stride=dil), :] = src[:, col:col + LANES]
        band.append(nat)

    for sub in range(tm // ts):
        rows = slice(sub * ts, (sub + 1) * ts)
        x = _group_tile(x_refs, bounds, rows)
        h = _rms(x, gmix_ref[...]).astype(BF16)
        gates = jax.nn.sigmoid(jnp.dot(h, wg_ref[...], preferred_element_type=F32))

        def rows_of(ref):
            if len(ref.shape) == 2:
                return ref[rows, :]
            return jnp.concatenate([ref[c, rows, :] for c in range(ref.shape[0])], axis=1)

        (o0, l0), (o1, l1), (o2, l2) = [(rows_of(o), rows_of(l)) for o, l in band]
        m = jnp.maximum(jnp.maximum(l0, l1), l2)
        e0, e1, e2 = jnp.exp(l0 - m), jnp.exp(l1 - m), jnp.exp(l2 - m)
        oa = (e0 * o0 + e1 * o1 + e2 * o2) * (1.0 / (e0 + e1 + e2))

        bra = jnp.dot(oa.astype(BF16), wa_ref[...], preferred_element_type=F32)
        brb = jnp.dot(ob_ref[rows, :].astype(BF16), wb_ref[...], preferred_element_type=F32)
        merged = (gates[:, :d] * bra + gates[:, d:] * brb).astype(BF16)
        x1 = x + jnp.dot(merged, wo_ref[...], preferred_element_type=F32)
        x1_ref[rows, :] = x1
        h2 = _rms(x1, gffn_ref[...])
        h2_hi = h2.astype(BF16)
        h2_ref[rows, :] = h2_hi
        h2lo_ref[rows, :] = (h2 - h2_hi.astype(F32)).astype(BF16)

    nt = (((1,), (1,)), ((), ()))
    hi_terms = lax.dot_general(wr_ref[...], h2_ref[...], nt, preferred_element_type=F32)
    logits = (hi_terms[:N_EXPERTS] + hi_terms[N_EXPERTS:]
              + lax.dot_general(wr_ref[:N_EXPERTS, :], h2lo_ref[...], nt, preferred_element_type=F32)
              + br_ref[...])
    eiota = lax.broadcasted_iota(I32, (N_EXPERTS, tm), 0)
    idxs, vals = [], []
    for _ in range(TOP_K):
        top = jnp.max(logits, axis=0, keepdims=True)
        idx = jnp.min(jnp.where(logits == top, eiota, N_EXPERTS), axis=0, keepdims=True)
        idxs.append(idx)
        vals.append(top)
        logits = jnp.where(eiota == idx, -jnp.inf, logits)
    ex = [jnp.exp(v - vals[0]) for v in vals]
    inv = 1.0 / (ex[0] + ex[1] + ex[2] + ex[3])
    gate_ref[...] = jnp.concatenate([v * inv for v in ex], axis=0)

    onehots = [eiota == idx for idx in idxs]
    multi = onehots[0] | onehots[1] | onehots[2] | onehots[3]
    multi_f = jnp.where(multi, 1.0, 0.0)
    prefix = jnp.dot(multi_f.astype(BF16), tri_ref[...], preferred_element_type=F32)
    cnt = jnp.broadcast_to(jnp.sum(multi_f, axis=1, keepdims=True), cnt_ref.shape)
    cnt_ref[...] = cnt.astype(I32)
    units = jnp.floor((cnt + (SEG_ALIGN - 1)) * (1.0 / SEG_ALIGN))
    lower = (lax.broadcasted_iota(I32, (N_EXPERTS, N_EXPERTS), 1)
             < lax.broadcasted_iota(I32, (N_EXPERTS, N_EXPERTS), 0))
    seg0 = jnp.dot(jnp.where(lower, 1.0, 0.0).astype(BF16), units.astype(BF16),
                   preferred_element_type=F32)[:, 0:1] * SEG_ALIGN
    pos_ref[...] = jnp.concatenate(
        [jnp.sum(jnp.where(oh, prefix + seg0, 0.0), axis=0, keepdims=True) for oh in onehots],
        axis=0).astype(I32)


def _merge_router(xs, band, ob, gmix, wg, wa, wb, wo, gffn, wr, br):
    d = xs[0].shape[-1]
    tm = TM_DISPATCH
    x_specs, bounds = _group_tile_specs(xs, tm)
    n = bounds[-1] * tm
    tri = jnp.asarray(np.triu(np.ones((tm, tm), np.float32), k=1), BF16)

    def row(width, dil=1):
        return pl.BlockSpec((tm // dil, dil * width), lambda i: (i, 0))

    def const(shape):
        return pl.BlockSpec(shape, lambda i: (0,) * len(shape))

    def tok4():
        return pl.BlockSpec((TOP_K, tm), lambda i: (0, i))

    band_specs = [row(QUAD, dil) for _, dil in DILATED_GROUPS for _ in range(2)]
    band_args = [a for pair in band for a in pair]
    return pl.pallas_call(
        functools.partial(_merge_kernel, bounds=bounds),
        grid=(n // tm,),
        in_specs=x_specs + band_specs + [row(NA_QUADS * QUAD),
                  const((1, d)), const(wg.shape), const(wa.shape), const(wb.shape), const(wo.shape),
                  const((1, d)), const(wr.shape), const(br.shape), const(tri.shape)],
        out_specs=[row(d), row(d), tok4(), tok4(),
                   pl.BlockSpec((None, N_EXPERTS, LANES), lambda i: (i, 0, 0))],
        out_shape=[jax.ShapeDtypeStruct((n, d), F32), jax.ShapeDtypeStruct((n, d), BF16),
                   jax.ShapeDtypeStruct((TOP_K, n), F32), jax.ShapeDtypeStruct((TOP_K, n), I32),
                   jax.ShapeDtypeStruct((n // tm, N_EXPERTS, LANES), I32)],
        scratch_shapes=[pltpu.VMEM((tm, d), BF16)]
                       + [pltpu.VMEM((QUAD // LANES, tm, LANES), F32)] * (2 * (N_DIL - 1)),
        compiler_params=_cparams("parallel"),
    )(*xs, *band_args, ob, gmix, wg, wa, wb, wo, gffn, wr, br, tri)


def _pack_bf16_pair(x):
    w = x.shape[1] // 2
    lo = pltpu.bitcast(x[:, :w], U32) >> 16
    hi = pltpu.bitcast(x[:, w:], U32) & jnp.uint32(0xFFFF0000)
    return lo | hi


def _unpack_bf16_pair(u):
    lo = pltpu.bitcast(u << 16, F32).astype(BF16)
    hi = pltpu.bitcast(u & jnp.uint32(0xFFFF0000), F32).astype(BF16)
    return lo, hi


def _pow2_copies(units, stage0, hbm0, max_units, make_copy, action):
    for b in range(max_units.bit_length() - 1, -1, -1):
        size = SEG_ALIGN << b

        @pl.when(((units >> b) & 1) == 1)
        def _():
            done = ((units >> (b + 1)) << (b + 1)) * SEG_ALIGN
            cp = make_copy(pl.multiple_of(stage0 + done, SEG_ALIGN),
                           pl.multiple_of(hbm0 + done, SEG_ALIGN), size)
            if action == "start":
                cp.start()
            else:
                cp.wait()


def _segment_copies(tile, c8_ref, off_ref, a_ref, make_copy, action):
    for e in range(N_EXPERTS):
        k = tile * N_EXPERTS + e
        _pow2_copies(c8_ref[k] // SEG_ALIGN, off_ref[k], a_ref[k], TM_DISPATCH // SEG_ALIGN,
                     make_copy, action)


def _segment_waits(tile, tot_ref, s_rows, make_copy):
    _pow2_copies(tot_ref[tile], 0, 0, s_rows // SEG_ALIGN, make_copy, "wait")


def _dispatch_kernel(c8_ref, off_ref, a_ref, tot_ref, tail_ref, h2_ref, pos_ref, xs_ref, stage_ref, sem):
    tile = pl.program_id(0)
    last = pl.num_programs(0) - 1
    slot = tile % 2
    s_rows, tm = stage_ref.shape[1], h2_ref.shape[0]

    def copier(buf):
        def make_copy(stage_row, hbm_row, size):
            return pltpu.make_async_copy(stage_ref.at[buf, pl.ds(stage_row, size)],
                                         xs_ref.at[pl.ds(hbm_row, size)], sem.at[buf])
        return make_copy

    @pl.when(tile == 0)
    def _():
        stage_ref[0, 0:EXPERT_BM] = jnp.zeros((EXPERT_BM, stage_ref.shape[2]), U32)
        for action in ("start", "wait"):
            for e in range(N_EXPERTS):
                _pow2_copies(tail_ref[N_EXPERTS + e], 0, tail_ref[e], EXPERT_BM // SEG_ALIGN - 1,
                             copier(0), action)

        def zero_block(b, carry):
            cp = copier(0)(0, pl.multiple_of(b * EXPERT_BM, EXPERT_BM), EXPERT_BM)
            cp.start()
            cp.wait()
            return carry
        lax.fori_loop(tail_ref[2 * N_EXPERTS], xs_ref.shape[0] // EXPERT_BM, zero_block, 0)

    pos = pos_ref[...]
    h2 = h2_ref[...]
    ck = ONEHOT_CHUNK
    for c in range(s_rows // ck):
        siota = lax.broadcasted_iota(I32, (ck, tm), 0) + c * ck
        compact = jnp.zeros((ck, tm), F32)
        for k in range(TOP_K):
            compact = jnp.where(siota == pos[k:k + 1, :], 1.0, compact)
        rows = jnp.dot(compact.astype(BF16), h2, preferred_element_type=F32)
        stage_ref[slot, c * ck:(c + 1) * ck, :] = _pack_bf16_pair(rows)

    _segment_copies(tile, c8_ref, off_ref, a_ref, copier(slot), "start")

    @pl.when(tile > 0)
    def _():
        _segment_waits(tile - 1, tot_ref, s_rows, copier(1 - slot))

    @pl.when(tile == last)
    def _():
        _segment_waits(tile, tot_ref, s_rows, copier(slot))


def _dispatch(c8, off, a, tot, tail, h2, pos, p_rows, s_rows):
    n, d = h2.shape
    tm = TM_DISPATCH
    grid_spec = pltpu.PrefetchScalarGridSpec(
        num_scalar_prefetch=5,
        grid=(n // tm,),
        in_specs=[pl.BlockSpec((tm, d), lambda i, *_: (i, 0)),
                  pl.BlockSpec((TOP_K, tm), lambda i, *_: (0, i))],
        out_specs=pl.BlockSpec(memory_space=pl.ANY),
        scratch_shapes=[pltpu.VMEM((2, s_rows, d // 2), U32), pltpu.SemaphoreType.DMA((2,))],
    )
    return pl.pallas_call(
        _dispatch_kernel,
        grid_spec=grid_spec,
        out_shape=jax.ShapeDtypeStruct((p_rows, d // 2), U32),
        compiler_params=_cparams("arbitrary"),
    )(c8, off, a, tot, tail, h2, pos)


def _expert_kernel(be_ref, nused_ref, xs_ref, wgu_ref, bgu_ref, wd_ref, bd_ref, ys_ref, wgu_bf, wd_bf):
    b = pl.program_id(0)
    active = b < nused_ref[0]
    new_expert = (b == 0) | (be_ref[b] != be_ref[jnp.maximum(b - 1, 0)])

    @pl.when(jnp.logical_not(active))
    def _():
        ys_ref[...] = jnp.zeros_like(ys_ref)

    @pl.when(active & new_expert)
    def _():
        wgu_bf[...] = wgu_ref[...].astype(BF16)
        wd_bf[...] = wd_ref[...].astype(BF16)

    @pl.when(active)
    def _():
        lo, hi = _unpack_bf16_pair(xs_ref[...])
        x = jnp.concatenate([lo, hi], axis=1)
        ck = EXPERT_CHUNK
        de = wd_bf.shape[0]
        y = None
        for c in range(de // ck):
            glu_cols, lin_cols = slice(c * ck, (c + 1) * ck), slice(de + c * ck, de + (c + 1) * ck)
            g = jnp.dot(x, wgu_bf[:, glu_cols], preferred_element_type=F32) + bgu_ref[:, glu_cols]
            u = jnp.dot(x, wgu_bf[:, lin_cols], preferred_element_type=F32) + bgu_ref[:, lin_cols]
            x_glu = jnp.minimum(g, SWIGLU_LIMIT)
            x_lin = jnp.clip(u, -SWIGLU_LIMIT, SWIGLU_LIMIT)
            act = x_glu * jax.nn.sigmoid(SWIGLU_ALPHA * x_glu) * (x_lin + 1.0)
            part = jnp.dot(act.astype(BF16), wd_bf[c * ck:(c + 1) * ck, :], preferred_element_type=F32)
            y = part if y is None else y + part
        y = y + bd_ref[...]
        ys_ref[...] = _pack_bf16_pair(y.astype(BF16).astype(F32))


def _expert_ffn(block_e, nused, xs, wgu, bgu, wd, bd):
    p_rows, half = xs.shape
    bm = EXPERT_BM
    d, de2 = wgu.shape[1], wgu.shape[2]

    def blk(b, be, nu):
        return (jnp.minimum(b, nu[0] - 1), 0)

    def wsel(b, be, nu):
        return (be[jnp.minimum(b, nu[0] - 1)], 0, 0)

    grid_spec = pltpu.PrefetchScalarGridSpec(
        num_scalar_prefetch=2,
        grid=(p_rows // bm,),
        in_specs=[pl.BlockSpec((bm, half), blk),
                  pl.BlockSpec((None, d, de2), wsel),
                  pl.BlockSpec((None, 1, de2), wsel),
                  pl.BlockSpec((None, de2 // 2, d), wsel),
                  pl.BlockSpec((None, 1, d), wsel)],
        out_specs=pl.BlockSpec((bm, half), lambda b, be, nu: (b, 0)),
        scratch_shapes=[pltpu.VMEM((d, de2), BF16), pltpu.VMEM((de2 // 2, d), BF16)],
    )
    return pl.pallas_call(
        _expert_kernel,
        grid_spec=grid_spec,
        out_shape=jax.ShapeDtypeStruct((p_rows, half), U32),
        compiler_params=_cparams("arbitrary"),
    )(block_e, nused, xs, wgu, bgu, wd, bd)


def _combine_kernel(c8_ref, off_ref, a_ref, tot_ref, ys_ref, x1_ref, pos_ref, gate_ref, g_ref, o_ref,
                    stage_ref, sem, *, tile0):
    step = pl.program_id(0)
    tile = step + tile0
    s_rows, tm = stage_ref.shape[0], x1_ref.shape[0]

    @pl.when(step == 0)
    def _():
        stage_ref[...] = jnp.zeros_like(stage_ref)

    def make_copy(stage_row, hbm_row, size):
        return pltpu.make_async_copy(ys_ref.at[pl.ds(hbm_row, size)],
                                     stage_ref.at[pl.ds(stage_row, size)], sem)

    _segment_copies(tile, c8_ref, off_ref, a_ref, make_copy, "start")
    pos = pos_ref[...]
    gate = gate_ref[...]
    ck = ONEHOT_CHUNK
    gmats = []
    for c in range(s_rows // ck):
        liota = lax.broadcasted_iota(I32, (tm, ck), 1) + c * ck
        gmat = jnp.zeros((tm, ck), F32)
        for k in range(TOP_K):
            gmat = jnp.where(liota == pos[:, k:k + 1], gate[:, k:k + 1], gmat)
        gmats.append(gmat.astype(BF16))
    _segment_waits(tile, tot_ref, s_rows, make_copy)
    moe_lo = moe_hi = None
    for c in range(s_rows // ck):
        lo, hi = _unpack_bf16_pair(stage_ref[c * ck:(c + 1) * ck, :])
        part_lo = jnp.dot(gmats[c], lo, preferred_element_type=F32)
        part_hi = jnp.dot(gmats[c], hi, preferred_element_type=F32)
        moe_lo = part_lo if moe_lo is None else moe_lo + part_lo
        moe_hi = part_hi if moe_hi is None else moe_hi + part_hi
    moe = jnp.concatenate([moe_lo, moe_hi], axis=1)
    o_ref[...] = _rms(x1_ref[...] + moe, g_ref[...])


def _combine(c8, off, a, tot, ys, x1, pos_t, gate_t, gfinal, tok0, batch, seq, s_rows):
    _, d = x1.shape
    tm = TM_DISPATCH
    tile0, per_seq = tok0 // tm, seq // tm

    def row(width):
        return pl.BlockSpec((tm, width), lambda i, *_: (i + tile0, 0))

    grid_spec = pltpu.PrefetchScalarGridSpec(
        num_scalar_prefetch=4,
        grid=(batch * per_seq,),
        in_specs=[pl.BlockSpec(memory_space=pl.ANY), row(d), row(TOP_K), row(TOP_K),
                  pl.BlockSpec((1, d), lambda i, *_: (0, 0))],
        out_specs=pl.BlockSpec((None, tm, d), lambda i, *_: (i // per_seq, i % per_seq, 0)),
        scratch_shapes=[pltpu.VMEM((s_rows, d // 2), U32), pltpu.SemaphoreType.DMA(())],
    )
    return pl.pallas_call(
        functools.partial(_combine_kernel, tile0=tile0),
        grid_spec=grid_spec,
        out_shape=jax.ShapeDtypeStruct((batch, seq, d), F32),
        compiler_params=_cparams("arbitrary"),
    )(c8, off, a, tot, ys, x1, pos_t, gate_t, gfinal)


def _routing_tables(cnt, p_rows):
    c8 = _round_up(cnt, SEG_ALIGN)
    off = jnp.cumsum(c8, axis=1) - c8
    tot = jnp.sum(c8, axis=0)
    padded = _round_up(tot, EXPERT_BM)
    pend = jnp.cumsum(padded)
    a = (pend - padded)[None, :] + jnp.cumsum(c8, axis=0) - c8
    n_blocks = p_rows // EXPERT_BM
    block_row = jnp.arange(n_blocks, dtype=I32) * EXPERT_BM
    block_e = jnp.sum((block_row[:, None] >= pend[None, :]).astype(I32), axis=1)
    block_e = jnp.minimum(block_e, N_EXPERTS - 1).astype(I32)
    nused = (pend[-1] // EXPERT_BM).astype(I32).reshape(1)
    tail = jnp.concatenate([pend - padded + tot, (padded - tot) // SEG_ALIGN, nused]).astype(I32)
    tile_units = (jnp.sum(c8, axis=1) // SEG_ALIGN).astype(I32)
    flat = lambda t: t.reshape(-1).astype(I32)
    return flat(c8), flat(off), flat(a), tile_units, tail, block_e, nused


def _permute_qkv_weight(w_in):
    d = w_in.shape[0]
    w = w_in.reshape(d, 3, N_QUADS, QUAD)
    scale = jnp.asarray([HEAD_DIM ** -0.5, 1.0, 1.0], F32).reshape(1, 3, 1, 1)
    return (w * scale).transpose(0, 2, 1, 3).reshape(d, N_QUADS * QKV_W).astype(BF16)


def kernel(x_prompt, x_sample, t5_rel_bias, norm_mix, w_in, na_rpb, w_branch_a, w_branch_b, w_gate, w_out,
           norm_ffn, w_router, b_router, w_gate_up, b_gate_up, w_down, b_down, norm_final):
    assert norm_mix.shape[0] == 1, "single-layer encoder"
    d = x_prompt.shape[-1]
    groups, tok0 = [], 0
    for xg in (x_prompt, x_sample):
        groups.append((xg.shape[0], xg.shape[1], tok0))
        tok0 += xg.shape[0] * xg.shape[1]
    n = tok0
    assert n % TM_QKV == 0 and n % TM_DISPATCH == 0 and all(g[2] % TM_DISPATCH == 0 for g in groups)
    xs_in = (x_prompt, x_sample)

    *qkv_band, qkv_na = _qkv_proj(xs_in, norm_mix[0][None], _permute_qkv_weight(w_in[0]))

    band = []
    for g, (window, dilation) in enumerate(DILATED_GROUPS):
        assert window // (2 * dilation) == BAND_HALF
        bias = _band_bias(t5_rel_bias[:, g * QUAD_HEADS:(g + 1) * QUAD_HEADS], dilation, BAND_BQ)
        band.append(_band_attention(qkv_band[g], dilation, bias, groups))
    ob = _neighbourhood_attention(qkv_na, _na_bias(na_rpb[0]), groups)

    wr = w_router[0].T
    wr_hi = wr.astype(BF16)
    wr_lo = (wr - wr_hi.astype(F32)).astype(BF16)
    x1, h2, gate, pos, cnt = _merge_router(
        xs_in, band, ob, norm_mix[0][None], w_gate[0].astype(BF16), w_branch_a[0].astype(BF16),
        w_branch_b[0].astype(BF16), w_out[0].astype(BF16), norm_ffn[0][None],
        jnp.concatenate([wr_hi, wr_lo], axis=0), b_router[0][:, None])

    n_tiles = n // TM_DISPATCH
    p_rows = _round_up(n * TOP_K + n_tiles * N_EXPERTS * (SEG_ALIGN - 1) + N_EXPERTS * (EXPERT_BM - 1),
                       EXPERT_BM)
    s_rows = _round_up(TM_DISPATCH * TOP_K + N_EXPERTS * (SEG_ALIGN - 1), 256)
    c8, off, a, tot, tail, block_e, nused = _routing_tables(cnt[:, :, 0], p_rows)

    xs = _dispatch(c8, off, a, tot, tail, h2, pos, p_rows, s_rows)
    ys = _expert_ffn(block_e, nused, xs, w_gate_up[0], b_gate_up[0][:, None, :], w_down[0],
                     b_down[0][:, None, :])

    pos_t, gate_t = pos.T, gate.T
    outs = []
    for (b, t, g0) in groups:
        outs.append(_combine(c8, off, a, tot, ys, x1, pos_t, gate_t, norm_final[None], g0, b, t, s_rows))
    return tuple(outs)
```

```python
import functools

import numpy as np
import jax
import jax.numpy as jnp
from jax import lax
from jax.experimental import pallas as pl
from jax.experimental.pallas import tpu as pltpu

F32 = jnp.float32
BF16 = jnp.bfloat16
I32 = jnp.int32
U32 = jnp.uint32

HEAD_DIM = 64
QUAD_HEADS = 4
QUAD = QUAD_HEADS * HEAD_DIM
LANES = 128
QKV_W = 3 * QUAD
DILATED_GROUPS = ((128, 1), (512, 4), (2048, 16))
N_DIL = len(DILATED_GROUPS)
NA_QUADS = 2
N_QUADS = N_DIL + NA_QUADS
BAND_HALF = 64
T5_BUCKETS = 32
T5_MAX_DISTANCE = 1024
GRID_W = 64
NA_KH = 8
NA_KW = 16
N_EXPERTS = 32
TOP_K = 4
SWIGLU_LIMIT = 7.0
SWIGLU_ALPHA = 1.702
RMS_EPS = 1e-6
NEG = -1e30

BAND_BQ = 128
BAND_OUTER = 512
NA_BAND_ROWS = 8
TM_QKV = 512
TM_DISPATCH = 512
TM_MERGE_SUB = 256
EXPERT_BM = 1024
EXPERT_CHUNK = 256
ONEHOT_CHUNK = 256
SEG_ALIGN = 8
VMEM_LIMIT = 56 * 1024 * 1024


def _round_up(x, m):
    return (x + m - 1) // m * m


def _cparams(*sem):
    return pltpu.CompilerParams(dimension_semantics=sem, vmem_limit_bytes=VMEM_LIMIT)


def _rms(x, g):
    ms = jnp.mean(x * x, axis=-1, keepdims=True)
    return x * lax.rsqrt(ms + RMS_EPS) * g


def _select_group(w, metas, fn):
    out = fn(metas[0], w - metas[0]["base"])
    for m in metas[1:]:
        cand = fn(m, w - m["base"])
        sel = w >= m["base"]
        out = jax.tree.map(lambda a, b: jnp.where(sel, b, a), out, cand)
    return out


def _group_tile_specs(xs, tm):
    specs, bounds, tile0 = [], [], 0
    for xg in xs:
        b, t, d = xg.shape
        assert t % tm == 0
        per_seq, n_tiles = t // tm, b * t // tm

        def im(i, *_, tile0=tile0, per_seq=per_seq, n_tiles=n_tiles):
            j = jnp.clip(i - tile0, 0, n_tiles - 1)
            return (j // per_seq, j % per_seq, 0)
        specs.append(pl.BlockSpec((None, tm, d), im))
        tile0 += n_tiles
        bounds.append(tile0)
    return specs, bounds


def _group_tile(refs, bounds, rows=slice(None)):
    i = pl.program_id(0)
    x = refs[-1][rows, :]
    for ref, bound in zip(refs[-2::-1], bounds[-2::-1]):
        x = jnp.where(i < bound, ref[rows, :], x)
    return x


def _qkv_kernel(*refs, bounds):
    ng = len(bounds)
    x_refs, (g_ref, w_ref), refs = refs[:ng], refs[ng:ng + 2], refs[ng + 2:]
    band_refs, na_ref, h_ref = refs[:N_DIL], refs[N_DIL], refs[N_DIL + 1]
    n_lane_blocks, tm, _ = h_ref.shape
    h_nat = _rms(_group_tile(x_refs, bounds), g_ref[...])
    for c in range(n_lane_blocks):
        h_ref[c] = h_nat[:, c * LANES:(c + 1) * LANES]

    def project(h, j):
        return jnp.dot(h.astype(BF16), w_ref[:, j * QKV_W:(j + 1) * QKV_W],
                       preferred_element_type=F32).astype(BF16)

    def class_rows(r, rows, dil):
        return jnp.concatenate([h_ref[c, pl.ds(r, rows, stride=dil), :] for c in range(n_lane_blocks)],
                               axis=1)

    for j in range(NA_QUADS):
        na_ref[j] = project(h_nat, N_DIL + j)
    for g, (_, dil) in enumerate(DILATED_GROUPS):
        rows = tm // dil
        if dil == 1:
            band_refs[g][...] = project(h_nat, g)
            continue
        hp = jnp.concatenate([class_rows(r, rows, dil) for r in range(dil)], axis=0)
        y = project(hp, g)
        for r in range(dil):
            band_refs[g][:, r * QKV_W:(r + 1) * QKV_W] = y[r * rows:(r + 1) * rows]


def _qkv_proj(xs, g, w):
    d = xs[0].shape[-1]
    tm = TM_QKV
    x_specs, bounds = _group_tile_specs(xs, tm)
    n = bounds[-1] * tm
    band_specs = [pl.BlockSpec((tm // dil, dil * QKV_W), lambda i: (i, 0)) for _, dil in DILATED_GROUPS]
    band_shapes = [jax.ShapeDtypeStruct((n // dil, dil * QKV_W), BF16) for _, dil in DILATED_GROUPS]
    return pl.pallas_call(
        functools.partial(_qkv_kernel, bounds=bounds),
        grid=(n // tm,),
        in_specs=x_specs + [pl.BlockSpec((1, d), lambda i: (0, 0)),
                            pl.BlockSpec((d, N_QUADS * QKV_W), lambda i: (0, 0))],
        out_specs=band_specs + [pl.BlockSpec((NA_QUADS, tm, QKV_W), lambda i: (0, i, 0))],
        out_shape=band_shapes + [jax.ShapeDtypeStruct((NA_QUADS, n, QKV_W), BF16)],
        scratch_shapes=[pltpu.VMEM((d // LANES, tm, LANES), F32)],
        compiler_params=_cparams("parallel"),
    )(*xs, g, w)


def _head_lane_masks(rows):
    lane_head = lax.broadcasted_iota(I32, (rows, QUAD), 1) // HEAD_DIM
    return [lane_head == h for h in range(QUAD_HEADS)]


def _stack_heads(q, masks):
    zero = jnp.zeros_like(q)
    return jnp.concatenate([jnp.where(m, q, zero) for m in masks], axis=0)


def _band_kernel(q_ref, kp_ref, kc_ref, kn_ref, vp_ref, vc_ref, vn_ref, b_ref, o_ref, l_ref,
                 *, decode, bq):
    n_sub = q_ref.shape[0] // bq
    t = decode(pl.program_id(0))
    first = (t["i"] == 0).astype(I32)
    last = (t["i"] == t["last"]).astype(I32)
    masks = _head_lane_masks(bq)
    k = jnp.concatenate([kp_ref[...], kc_ref[...], kn_ref[...]], axis=0)
    v = jnp.concatenate([vp_ref[...], vc_ref[...], vn_ref[...]], axis=0)
    for j in range(n_sub):
        rows = slice(j * bq, (j + 1) * bq)
        keys = slice(j * bq, (j + 1) * bq + 2 * BAND_HALF)
        variant = (first if j == 0 else 0) + (2 * last if j == n_sub - 1 else 0)
        qs = _stack_heads(q_ref[rows, :], masks)
        s = lax.dot_general(qs, k[keys], (((1,), (1,)), ((), ())), preferred_element_type=F32)
        s = s + b_ref[variant]
        m = jnp.max(s, axis=-1, keepdims=True)
        p = jnp.exp(s - m)
        l = jnp.sum(p, axis=-1, keepdims=True)
        pv = jnp.dot(p.astype(BF16), v[keys], preferred_element_type=F32) * (1.0 / l)
        lse = jnp.broadcast_to(m + jnp.log(l), pv.shape)
        o = jnp.zeros((bq, QUAD), F32)
        le = jnp.zeros((bq, QUAD), F32)
        for h in range(QUAD_HEADS):
            o = jnp.where(masks[h], pv[h * bq:(h + 1) * bq], o)
            le = jnp.where(masks[h], lse[h * bq:(h + 1) * bq], le)
        o_ref[rows, :] = o
        l_ref[rows, :] = le


def _t5_bucket(rel):
    nb = T5_BUCKETS // 2
    ret = (rel > 0).astype(np.int32) * nb
    n = np.abs(rel)
    max_exact = nb // 2
    large = max_exact + (np.log(np.maximum(n, 1) / max_exact)
                         / np.log(T5_MAX_DISTANCE / max_exact) * (nb - max_exact)).astype(np.int32)
    large = np.minimum(large, nb - 1)
    return ret + np.where(n < max_exact, n, large)


def _select_rows(table, index, n_rows):
    onehot = np.zeros((index.size, n_rows), np.float32)
    onehot[np.arange(index.size), index.reshape(-1)] = 1.0
    out = jnp.einsum("nb,b...->n...", jnp.asarray(onehot), table.astype(F32),
                     precision=lax.Precision.HIGHEST)
    return out.reshape(index.shape + table.shape[1:])


def _band_bias(table, dilation, bq):
    nk = bq + 2 * BAND_HALF
    qi = np.arange(bq)[:, None]
    c = np.arange(nk)[None, :]
    rel = c - BAND_HALF - qi
    band = np.abs(rel) <= BAND_HALF
    bias = jnp.transpose(_select_rows(table, _t5_bucket(dilation * rel), T5_BUCKETS), (2, 0, 1))
    out = []
    for variant in range(4):
        valid = band
        if variant & 1:
            valid = valid & (c >= BAND_HALF)
        if variant & 2:
            valid = valid & (c < bq + BAND_HALF)
        out.append(jnp.where(valid[None], bias, NEG).reshape(QUAD_HEADS * bq, nk))
    return jnp.stack(out, axis=0)


def _band_attention(a, dilation, bias, groups):
    d, bq = dilation, BAND_BQ
    n = a.shape[0] * d
    bo = min([BAND_OUTER] + [t // d for (_, t, _) in groups])
    hq = bo // BAND_HALF
    metas, base = [], 0
    for (b, t, tok0) in groups:
        seq = t // d
        assert bo % bq == 0 and seq % bo == 0 and tok0 % (d * bo) == 0
        nb = seq // bo
        metas.append(dict(base=base, nb=nb, rb0=tok0 // d // bo))
        base += b * d * nb
    n_items = base

    def decode(w):
        def f(m, wl):
            nb = m["nb"]
            s = wl // (d * nb)
            r = (wl // nb) % d
            i = wl % nb
            seq0 = m["rb0"] + s * nb
            return dict(r=r, i=i, seq0=seq0, last=nb - 1 + 0 * i)
        return _select_group(w, metas, f)

    def cur(col):
        def im(w):
            t = decode(w)
            return (t["seq0"] + t["i"], 3 * t["r"] + col)
        return pl.BlockSpec((bo, QUAD), im)

    def prev(col):
        def im(w):
            t = decode(w)
            return (hq * t["seq0"] + jnp.maximum(hq * t["i"] - 1, 0), 3 * t["r"] + col)
        return pl.BlockSpec((BAND_HALF, QUAD), im)

    def nxt(col):
        def im(w):
            t = decode(w)
            j = jnp.minimum(hq * t["i"] + hq, hq * t["last"] + hq - 1)
            return (hq * t["seq0"] + j, 3 * t["r"] + col)
        return pl.BlockSpec((BAND_HALF, QUAD), im)

    def out_im(w):
        t = decode(w)
        return (t["seq0"] + t["i"], t["r"])

    return pl.pallas_call(
        functools.partial(_band_kernel, decode=decode, bq=bq),
        grid=(n_items,),
        in_specs=[cur(0), prev(1), cur(1), nxt(1), prev(2), cur(2), nxt(2),
                  pl.BlockSpec(bias.shape, lambda w: (0, 0, 0))],
        out_specs=[pl.BlockSpec((bo, QUAD), out_im), pl.BlockSpec((bo, QUAD), out_im)],
        out_shape=[jax.ShapeDtypeStruct((n // d, d * QUAD), F32)] * 2,
        compiler_params=_cparams("parallel"),
    )(a, a, a, a, a, a, a, bias)


def _na_bias(rpb):
    qc = np.arange(GRID_W)[:, None]
    kc = np.arange(GRID_W)[None, :]
    col_idx = np.clip(kc - qc + NA_KW - 1, 0, 2 * NA_KW - 2)
    ws = np.clip(qc - NA_KW // 2, 0, GRID_W - NA_KW)
    mask = (kc >= ws) & (kc < ws + NA_KW)
    t = _select_rows(jnp.moveaxis(rpb, 2, 0), col_idx, 2 * NA_KW - 1)
    t = jnp.where(mask[:, :, None, None], t, NEG)
    b = jnp.stack([t[..., dv:dv + NA_KH] for dv in range(NA_KH)], axis=0)
    b = b.reshape(NA_KH, GRID_W, GRID_W, NA_QUADS, QUAD_HEADS, NA_KH)
    b = b.transpose(3, 0, 4, 1, 5, 2)
    return b.reshape(NA_QUADS, NA_KH, QUAD_HEADS * GRID_W, NA_KH * GRID_W)


def _na_kernel(q_ref, kp_ref, kc_ref, kn_ref, vp_ref, vc_ref, vn_ref, b_ref, o_ref,
               kbuf, vbuf, *, metas, items_per_quad):
    blk = NA_BAND_ROWS * GRID_W
    w = pl.program_id(0) % items_per_quad
    t = _select_group(w, metas, lambda m, wl: dict(i=wl % m["nbands"], nbands=m["nbands"] + 0 * wl))
    band, rows = t["i"], t["nbands"] * NA_BAND_ROWS
    kbuf[0:blk] = kp_ref[...]
    kbuf[blk:2 * blk] = kc_ref[...]
    kbuf[2 * blk:3 * blk] = kn_ref[...]
    vbuf[0:blk] = vp_ref[...]
    vbuf[blk:2 * blk] = vc_ref[...]
    vbuf[2 * blk:3 * blk] = vn_ref[...]
    masks = _head_lane_masks(GRID_W)

    for j in range(NA_BAND_ROWS):
        r = band * NA_BAND_ROWS + j
        rs = jnp.clip(r - NA_KH // 2, 0, rows - NA_KH)
        loc = pl.multiple_of((rs - (band - 1) * NA_BAND_ROWS) * GRID_W, GRID_W)
        kw = kbuf[pl.ds(loc, NA_KH * GRID_W), :]
        vw = vbuf[pl.ds(loc, NA_KH * GRID_W), :]
        qrow = slice(j * GRID_W, (j + 1) * GRID_W)
        qs = _stack_heads(q_ref[qrow, :], masks)
        s = lax.dot_general(qs, kw, (((1,), (1,)), ((), ())), preferred_element_type=F32)
        s = s + b_ref[rs - r + NA_KH - 1]
        m = jnp.max(s, axis=-1, keepdims=True)
        p = jnp.exp(s - m)
        l = jnp.sum(p, axis=-1, keepdims=True)
        pv = jnp.dot(p.astype(BF16), vw, preferred_element_type=F32) * (1.0 / l)
        o = jnp.zeros((GRID_W, QUAD), F32)
        for h in range(QUAD_HEADS):
            o = jnp.where(masks[h], pv[h * GRID_W:(h + 1) * GRID_W], o)
        o_ref[qrow, :] = o


def _neighbourhood_attention(qkv, bias, groups):
    _, n, _ = qkv.shape
    blk = NA_BAND_ROWS * GRID_W
    metas, base = [], 0
    for (b, t, tok0) in groups:
        rows = t // GRID_W
        assert t % GRID_W == 0 and rows % NA_BAND_ROWS == 0 and rows >= NA_KH and tok0 % blk == 0
        nbands = rows // NA_BAND_ROWS
        metas.append(dict(base=base, nbands=nbands, tb0=tok0 // blk))
        base += b * nbands
    items_per_quad = base

    def decode(w):
        quad = w // items_per_quad
        wq = w % items_per_quad

        def f(m, wl):
            s = wl // m["nbands"]
            i = wl % m["nbands"]
            return dict(i=i, seq0=m["tb0"] + s * m["nbands"], last=m["nbands"] - 1 + 0 * i)
        t = _select_group(wq, metas, f)
        t["quad"] = quad
        return t

    def spec(col, shift):
        def im(w):
            t = decode(w)
            i = jnp.clip(t["i"] + shift, 0, t["last"])
            return (t["quad"], t["seq0"] + i, col)
        return pl.BlockSpec((None, blk, QUAD), im)

    def out_im(w):
        t = decode(w)
        return (t["seq0"] + t["i"], t["quad"])

    return pl.pallas_call(
        functools.partial(_na_kernel, metas=metas, items_per_quad=items_per_quad),
        grid=(NA_QUADS * items_per_quad,),
        in_specs=[spec(0, 0), spec(1, -1), spec(1, 0), spec(1, 1), spec(2, -1), spec(2, 0), spec(2, 1),
                  pl.BlockSpec((None, NA_KH, QUAD_HEADS * GRID_W, NA_KH * GRID_W),
                               lambda w: (w // items_per_quad, 0, 0, 0))],
        out_specs=pl.BlockSpec((blk, QUAD), out_im),
        out_shape=jax.ShapeDtypeStruct((n, NA_QUADS * QUAD), F32),
        scratch_shapes=[pltpu.VMEM((3 * blk, QUAD), BF16), pltpu.VMEM((3 * blk, QUAD), BF16)],
        compiler_params=_cparams("parallel"),
    )(qkv, qkv, qkv, qkv, qkv, qkv, qkv, bias)


def _merge_kernel(*refs, bounds):
    ng = len(bounds)
    x_refs, refs = refs[:ng], refs[ng:]
    (o0_ref, l0_ref, o1_ref, l1_ref, o2_ref, l2_ref, ob_ref,
     gmix_ref, wg_ref, wa_ref, wb_ref, wo_ref, gffn_ref, wr_ref, br_ref, tri_ref,
     x1_ref, h2_ref, gate_ref, pos_ref, cnt_ref, h2lo_ref, *nat_refs) = refs
    tm, d = x1_ref.shape
    ts = TM_MERGE_SUB

    band = [(o0_ref, l0_ref)]
    pairs = ((o1_ref, l1_ref), (o2_ref, l2_ref))
    for g, (_, dil) in enumerate(DILATED_GROUPS[1:]):
        nat = nat_refs[2 * g:2 * g + 2]
        for src, dst in zip(pairs[g], nat):
            for r in range(dil):
                for c in range(QUAD // LANES):
                    col = r * QUAD + c * LANES
                    dst[c, pl.ds(r, tm // dil, stride=dil), :] = src[:, col:col + LANES]
        band.append(nat)

    for sub in range(tm // ts):
        rows = slice(sub * ts, (sub + 1) * ts)
        x = _group_tile(x_refs, bounds, rows)
        h = _rms(x, gmix_ref[...]).astype(BF16)
        gates = jax.nn.sigmoid(jnp.dot(h, wg_ref[...], preferred_element_type=F32))

        def rows_of(ref):
            if len(ref.shape) == 2:
                return ref[rows, :]
            return jnp.concatenate([ref[c, rows, :] for c in range(ref.shape[0])], axis=1)

        (o0, l0), (o1, l1), (o2, l2) = [(rows_of(o), rows_of(l)) for o, l in band]
        m = jnp.maximum(jnp.maximum(l0, l1), l2)
        e0, e1, e2 = jnp.exp(l0 - m), jnp.exp(l1 - m), jnp.exp(l2 - m)
        oa = (e0 * o0 + e1 * o1 + e2 * o2) * (1.0 / (e0 + e1 + e2))

        bra = jnp.dot(oa.astype(BF16), wa_ref[...], preferred_element_type=F32)
        brb = jnp.dot(ob_ref[rows, :].astype(BF16), wb_ref[...], preferred_element_type=F32)
        merged = (gates[:, :d] * bra + gates[:, d:] * brb).astype(BF16)
        x1 = x + jnp.dot(merged, wo_ref[...], preferred_element_type=F32)
        x1_ref[rows, :] = x1
        h2 = _rms(x1, gffn_ref[...])
        h2_hi = h2.astype(BF16)
        h2_ref[rows, :] = h2_hi
        h2lo_ref[rows, :] = (h2 - h2_hi.astype(F32)).astype(BF16)

    nt = (((1,), (1,)), ((), ()))
    hi_terms = lax.dot_general(wr_ref[...], h2_ref[...], nt, preferred_element_type=F32)
    logits = (hi_terms[:N_EXPERTS] + hi_terms[N_EXPERTS:]
              + lax.dot_general(wr_ref[:N_EXPERTS, :], h2lo_ref[...], nt, preferred_element_type=F32)
              + br_ref[...])
    eiota = lax.broadcasted_iota(I32, (N_EXPERTS, tm), 0)
    idxs, vals = [], []
    for _ in range(TOP_K):
        top = jnp.max(logits, axis=0, keepdims=True)
        idx = jnp.min(jnp.where(logits == top, eiota, N_EXPERTS), axis=0, keepdims=True)
        idxs.append(idx)
        vals.append(top)
        logits = jnp.where(eiota == idx, -jnp.inf, logits)
    ex = [jnp.exp(v - vals[0]) for v in vals]
    inv = 1.0 / (ex[0] + ex[1] + ex[2] + ex[3])
    gate_ref[...] = jnp.concatenate([v * inv for v in ex], axis=0)

    onehots = [eiota == idx for idx in idxs]
    multi = onehots[0] | onehots[1] | onehots[2] | onehots[3]
    multi_f = jnp.where(multi, 1.0, 0.0)
    prefix = jnp.dot(multi_f.astype(BF16), tri_ref[...], preferred_element_type=F32)
    cnt = jnp.broadcast_to(jnp.sum(multi_f, axis=1, keepdims=True), cnt_ref.shape)
    cnt_ref[...] = cnt.astype(I32)
    units = jnp.floor((cnt + (SEG_ALIGN - 1)) * (1.0 / SEG_ALIGN))
    lower = (lax.broadcasted_iota(I32, (N_EXPERTS, N_EXPERTS), 1)
             < lax.broadcasted_iota(I32, (N_EXPERTS, N_EXPERTS), 0))
    seg0 = jnp.dot(jnp.where(lower, 1.0, 0.0).astype(BF16), units.astype(BF16),
                   preferred_element_type=F32)[:, 0:1] * SEG_ALIGN
    pos_ref[...] = jnp.concatenate(
        [jnp.sum(jnp.where(oh, prefix + seg0, 0.0), axis=0, keepdims=True) for oh in onehots],
        axis=0).astype(I32)


def _merge_router(xs, band, ob, gmix, wg, wa, wb, wo, gffn, wr, br):
    d = xs[0].shape[-1]
    tm = TM_DISPATCH
    x_specs, bounds = _group_tile_specs(xs, tm)
    n = bounds[-1] * tm
    tri = jnp.asarray(np.triu(np.ones((tm, tm), np.float32), k=1), BF16)

    def row(width, dil=1):
        return pl.BlockSpec((tm // dil, dil * width), lambda i: (i, 0))

    def const(shape):
        return pl.BlockSpec(shape, lambda i: (0,) * len(shape))

    def tok4():
        return pl.BlockSpec((TOP_K, tm), lambda i: (0, i))

    band_specs = [row(QUAD, dil) for _, dil in DILATED_GROUPS for _ in range(2)]
    band_args = [a for pair in band for a in pair]
    return pl.pallas_call(
        functools.partial(_merge_kernel, bounds=bounds),
        grid=(n // tm,),
        in_specs=x_specs + band_specs + [row(NA_QUADS * QUAD),
                  const((1, d)), const(wg.shape), const(wa.shape), const(wb.shape), const(wo.shape),
                  const((1, d)), const(wr.shape), const(br.shape), const(tri.shape)],
        out_specs=[row(d), row(d), tok4(), tok4(),
                   pl.BlockSpec((None, N_EXPERTS, LANES), lambda i: (i, 0, 0))],
        out_shape=[jax.ShapeDtypeStruct((n, d), F32), jax.ShapeDtypeStruct((n, d), BF16),
                   jax.ShapeDtypeStruct((TOP_K, n), F32), jax.ShapeDtypeStruct((TOP_K, n), I32),
                   jax.ShapeDtypeStruct((n // tm, N_EXPERTS, LANES), I32)],
        scratch_shapes=[pltpu.VMEM((tm, d), BF16)]
                       + [pltpu.VMEM((QUAD // LANES, tm, LANES), F32)] * (2 * (N_DIL - 1)),
        compiler_params=_cparams("parallel"),
    )(*xs, *band_args, ob, gmix, wg, wa, wb, wo, gffn, wr, br, tri)


def _pack_bf16_pair(x):
    w = x.shape[1] // 2
    lo = pltpu.bitcast(x[:, :w], U32) >> 16
    hi = pltpu.bitcast(x[:, w:], U32) & jnp.uint32(0xFFFF0000)
    return lo | hi


def _unpack_bf16_pair(u):
    lo = pltpu.bitcast(u << 16, F32).astype(BF16)
    hi = pltpu.bitcast(u & jnp.uint32(0xFFFF0000), F32).astype(BF16)
    return lo, hi


def _pow2_copies(units, stage0, hbm0, max_units, make_copy, action):
    for b in range(max_units.bit_length() - 1, -1, -1):
        size = SEG_ALIGN << b

        @pl.when(((units >> b) & 1) == 1)
        def _():
            done = ((units >> (b + 1)) << (b + 1)) * SEG_ALIGN
            cp = make_copy(pl.multiple_of(stage0 + done, SEG_ALIGN),
                           pl.multiple_of(hbm0 + done, SEG_ALIGN), size)
            if action == "start":
                cp.start()
            else:
                cp.wait()


def _segment_copies(tile, c8_ref, off_ref, a_ref, make_copy, action):
    for e in range(N_EXPERTS):
        k = tile * N_EXPERTS + e
        _pow2_copies(c8_ref[k] // SEG_ALIGN, off_ref[k], a_ref[k], TM_DISPATCH // SEG_ALIGN,
                     make_copy, action)


def _segment_waits(tile, tot_ref, s_rows, make_copy):
    _pow2_copies(tot_ref[tile], 0, 0, s_rows // SEG_ALIGN, make_copy, "wait")


def _dispatch_kernel(c8_ref, off_ref, a_ref, tot_ref, tail_ref, h2_ref, pos_ref, xs_ref, stage_ref, sem):
    tile = pl.program_id(0)
    last = pl.num_programs(0) - 1
    slot = tile % 2
    s_rows, tm = stage_ref.shape[1], h2_ref.shape[0]

    def copier(buf):
        def make_copy(stage_row, hbm_row, size):
            return pltpu.make_async_copy(stage_ref.at[buf, pl.ds(stage_row, size)],
                                         xs_ref.at[pl.ds(hbm_row, size)], sem.at[buf])
        return make_copy

    @pl.when(tile == 0)
    def _():
        stage_ref[0, 0:EXPERT_BM] = jnp.zeros((EXPERT_BM, stage_ref.shape[2]), U32)
        for action in ("start", "wait"):
            for e in range(N_EXPERTS):
                _pow2_copies(tail_ref[N_EXPERTS + e], 0, tail_ref[e], EXPERT_BM // SEG_ALIGN - 1,
                             copier(0), action)

        def zero_block(b, carry):
            cp = copier(0)(0, pl.multiple_of(b * EXPERT_BM, EXPERT_BM), EXPERT_BM)
            cp.start()
            cp.wait()
            return carry
        lax.fori_loop(tail_ref[2 * N_EXPERTS], xs_ref.shape[0] // EXPERT_BM, zero_block, 0)

    pos = pos_ref[...]
    h2 = h2_ref[...]
    ck = ONEHOT_CHUNK
    for c in range(s_rows // ck):
        siota = lax.broadcasted_iota(I32, (ck, tm), 0) + c * ck
        compact = jnp.zeros((ck, tm), F32)
        for k in range(TOP_K):
            compact = jnp.where(siota == pos[k:k + 1, :], 1.0, compact)
        rows = jnp.dot(compact.astype(BF16), h2, preferred_element_type=F32)
        stage_ref[slot, c * ck:(c + 1) * ck, :] = _pack_bf16_pair(rows)

    _segment_copies(tile, c8_ref, off_ref, a_ref, copier(slot), "start")

    @pl.when(tile > 0)
    def _():
        _segment_waits(tile - 1, tot_ref, s_rows, copier(1 - slot))

    @pl.when(tile == last)
    def _():
        _segment_waits(tile, tot_ref, s_rows, copier(slot))


def _dispatch(c8, off, a, tot, tail, h2, pos, p_rows, s_rows):
    n, d = h2.shape
    tm = TM_DISPATCH
    grid_spec = pltpu.PrefetchScalarGridSpec(
        num_scalar_prefetch=5,
        grid=(n // tm,),
        in_specs=[pl.BlockSpec((tm, d), lambda i, *_: (i, 0)),
                  pl.BlockSpec((TOP_K, tm), lambda i, *_: (0, i))],
        out_specs=pl.BlockSpec(memory_space=pl.ANY),
        scratch_shapes=[pltpu.VMEM((2, s_rows, d // 2), U32), pltpu.SemaphoreType.DMA((2,))],
    )
    return pl.pallas_call(
        _dispatch_kernel,
        grid_spec=grid_spec,
        out_shape=jax.ShapeDtypeStruct((p_rows, d // 2), U32),
        compiler_params=_cparams("arbitrary"),
    )(c8, off, a, tot, tail, h2, pos)


def _expert_kernel(be_ref, nused_ref, xs_ref, wgu_ref, bgu_ref, wd_ref, bd_ref, ys_ref, wgu_bf, wd_bf):
    b = pl.program_id(0)
    active = b < nused_ref[0]
    new_expert = (b == 0) | (be_ref[b] != be_ref[jnp.maximum(b - 1, 0)])

    @pl.when(jnp.logical_not(active))
    def _():
        ys_ref[...] = jnp.zeros_like(ys_ref)

    @pl.when(active & new_expert)
    def _():
        wgu_bf[...] = wgu_ref[...].astype(BF16)
        wd_bf[...] = wd_ref[...].astype(BF16)

    @pl.when(active)
    def _():
        lo, hi = _unpack_bf16_pair(xs_ref[...])
        x = jnp.concatenate([lo, hi], axis=1)
        ck = EXPERT_CHUNK
        de = wd_bf.shape[0]
        y = None
        for c in range(de // ck):
            glu_cols, lin_cols = slice(c * ck, (c + 1) * ck), slice(de + c * ck, de + (c + 1) * ck)
            g = jnp.dot(x, wgu_bf[:, glu_cols], preferred_element_type=F32) + bgu_ref[:, glu_cols]
            u = jnp.dot(x, wgu_bf[:, lin_cols], preferred_element_type=F32) + bgu_ref[:, lin_cols]
            x_glu = jnp.minimum(g, SWIGLU_LIMIT)
            x_lin = jnp.clip(u, -SWIGLU_LIMIT, SWIGLU_LIMIT)
            act = x_glu * jax.nn.sigmoid(SWIGLU_ALPHA * x_glu) * (x_lin + 1.0)
            part = jnp.dot(act.astype(BF16), wd_bf[c * ck:(c + 1) * ck, :], preferred_element_type=F32)
            y = part if y is None else y + part
        y = y + bd_ref[...]
        ys_ref[...] = _pack_bf16_pair(y.astype(BF16).astype(F32))


def _expert_ffn(block_e, nused, xs, wgu, bgu, wd, bd):
    p_rows, half = xs.shape
    bm = EXPERT_BM
    d, de2 = wgu.shape[1], wgu.shape[2]

    def blk(b, be, nu):
        return (jnp.minimum(b, nu[0] - 1), 0)

    def wsel(b, be, nu):
        return (be[jnp.minimum(b, nu[0] - 1)], 0, 0)

    grid_spec = pltpu.PrefetchScalarGridSpec(
        num_scalar_prefetch=2,
        grid=(p_rows // bm,),
        in_specs=[pl.BlockSpec((bm, half), blk),
                  pl.BlockSpec((None, d, de2), wsel),
                  pl.BlockSpec((None, 1, de2), wsel),
                  pl.BlockSpec((None, de2 // 2, d), wsel),
                  pl.BlockSpec((None, 1, d), wsel)],
        out_specs=pl.BlockSpec((bm, half), lambda b, be, nu: (b, 0)),
        scratch_shapes=[pltpu.VMEM((d, de2), BF16), pltpu.VMEM((de2 // 2, d), BF16)],
    )
    return pl.pallas_call(
        _expert_kernel,
        grid_spec=grid_spec,
        out_shape=jax.ShapeDtypeStruct((p_rows, half), U32),
        compiler_params=_cparams("arbitrary"),
    )(block_e, nused, xs, wgu, bgu, wd, bd)


def _combine_kernel(c8_ref, off_ref, a_ref, tot_ref, ys_ref, x1_ref, pos_ref, gate_ref, g_ref, o_ref,
                    stage_ref, sem, *, tile0):
    step = pl.program_id(0)
    tile = step + tile0
    slot = step % 2
    s_rows, tm = stage_ref.shape[1], x1_ref.shape[0]

    def copier(buf):
        def make_copy(stage_row, hbm_row, size):
            return pltpu.make_async_copy(ys_ref.at[pl.ds(hbm_row, size)],
                                         stage_ref.at[buf, pl.ds(stage_row, size)], sem.at[buf])
        return make_copy

    @pl.when(step == 0)
    def _():
        stage_ref[...] = jnp.zeros_like(stage_ref)
        _segment_copies(tile, c8_ref, off_ref, a_ref, copier(0), "start")

    @pl.when(step + 1 < pl.num_programs(0))
    def _():
        _segment_copies(tile + 1, c8_ref, off_ref, a_ref, copier(1 - slot), "start")

    _segment_waits(tile, tot_ref, s_rows, copier(slot))
    pos = pos_ref[...]
    gate = gate_ref[...]
    ck = ONEHOT_CHUNK
    moe_lo = moe_hi = None
    for c in range(s_rows // ck):
        liota = lax.broadcasted_iota(I32, (tm, ck), 1) + c * ck
        gmat = jnp.zeros((tm, ck), F32)
        for k in range(TOP_K):
            gmat = jnp.where(liota == pos[:, k:k + 1], gate[:, k:k + 1], gmat)
        gmat = gmat.astype(BF16)
        lo, hi = _unpack_bf16_pair(stage_ref[slot, c * ck:(c + 1) * ck, :])
        part_lo = jnp.dot(gmat, lo, preferred_element_type=F32)
        part_hi = jnp.dot(gmat, hi, preferred_element_type=F32)
        moe_lo = part_lo if moe_lo is None else moe_lo + part_lo
        moe_hi = part_hi if moe_hi is None else moe_hi + part_hi
    moe = jnp.concatenate([moe_lo, moe_hi], axis=1)
    o_ref[...] = _rms(x1_ref[...] + moe, g_ref[...])


def _combine(c8, off, a, tot, ys, x1, pos_t, gate_t, gfinal, tok0, batch, seq, s_rows):
    _, d = x1.shape
    tm = TM_DISPATCH
    tile0, per_seq = tok0 // tm, seq // tm

    def row(width):
        return pl.BlockSpec((tm, width), lambda i, *_: (i + tile0, 0))

    grid_spec = pltpu.PrefetchScalarGridSpec(
        num_scalar_prefetch=4,
        grid=(batch * per_seq,),
        in_specs=[pl.BlockSpec(memory_space=pl.ANY), row(d), row(TOP_K), row(TOP_K),
                  pl.BlockSpec((1, d), lambda i, *_: (0, 0))],
        out_specs=pl.BlockSpec((None, tm, d), lambda i, *_: (i // per_seq, i % per_seq, 0)),
        scratch_shapes=[pltpu.VMEM((2, s_rows, d // 2), U32), pltpu.SemaphoreType.DMA((2,))],
    )
    return pl.pallas_call(
        functools.partial(_combine_kernel, tile0=tile0),
        grid_spec=grid_spec,
        out_shape=jax.ShapeDtypeStruct((batch, seq, d), F32),
        compiler_params=_cparams("arbitrary"),
    )(c8, off, a, tot, ys, x1, pos_t, gate_t, gfinal)


def _routing_tables(cnt, p_rows):
    c8 = _round_up(cnt, SEG_ALIGN)
    off = jnp.cumsum(c8, axis=1) - c8
    tot = jnp.sum(c8, axis=0)
    padded = _round_up(tot, EXPERT_BM)
    pend = jnp.cumsum(padded)
    a = (pend - padded)[None, :] + jnp.cumsum(c8, axis=0) - c8
    n_blocks = p_rows // EXPERT_BM
    block_row = jnp.arange(n_blocks, dtype=I32) * EXPERT_BM
    block_e = jnp.sum((block_row[:, None] >= pend[None, :]).astype(I32), axis=1)
    block_e = jnp.minimum(block_e, N_EXPERTS - 1).astype(I32)
    nused = (pend[-1] // EXPERT_BM).astype(I32).reshape(1)
    tail = jnp.concatenate([pend - padded + tot, (padded - tot) // SEG_ALIGN, nused]).astype(I32)
    tile_units = (jnp.sum(c8, axis=1) // SEG_ALIGN).astype(I32)
    flat = lambda t: t.reshape(-1).astype(I32)
    return flat(c8), flat(off), flat(a), tile_units, tail, block_e, nused


def _permute_qkv_weight(w_in):
    d = w_in.shape[0]
    w = w_in.reshape(d, 3, N_QUADS, QUAD)
    scale = jnp.asarray([HEAD_DIM ** -0.5, 1.0, 1.0], F32).reshape(1, 3, 1, 1)
    return (w * scale).transpose(0, 2, 1, 3).reshape(d, N_QUADS * QKV_W).astype(BF16)


def kernel(x_prompt, x_sample, t5_rel_bias, norm_mix, w_in, na_rpb, w_branch_a, w_branch_b, w_gate, w_out,
           norm_ffn, w_router, b_router, w_gate_up, b_gate_up, w_down, b_down, norm_final):
    assert norm_mix.shape[0] == 1, "single-layer encoder"
    d = x_prompt.shape[-1]
    groups, tok0 = [], 0
    for xg in (x_prompt, x_sample):
        groups.append((xg.shape[0], xg.shape[1], tok0))
        tok0 += xg.shape[0] * xg.shape[1]
    n = tok0
    assert n % TM_QKV == 0 and n % TM_DISPATCH == 0 and all(g[2] % TM_DISPATCH == 0 for g in groups)
    xs_in = (x_prompt, x_sample)

    *qkv_band, qkv_na = _qkv_proj(xs_in, norm_mix[0][None], _permute_qkv_weight(w_in[0]))

    band = []
    for g, (window, dilation) in enumerate(DILATED_GROUPS):
        assert window // (2 * dilation) == BAND_HALF
        bias = _band_bias(t5_rel_bias[:, g * QUAD_HEADS:(g + 1) * QUAD_HEADS], dilation, BAND_BQ)
        band.append(_band_attention(qkv_band[g], dilation, bias, groups))
    ob = _neighbourhood_attention(qkv_na, _na_bias(na_rpb[0]), groups)

    wr = w_router[0].T
    wr_hi = wr.astype(BF16)
    wr_lo = (wr - wr_hi.astype(F32)).astype(BF16)
    x1, h2, gate, pos, cnt = _merge_router(
        xs_in, band, ob, norm_mix[0][None], w_gate[0].astype(BF16), w_branch_a[0].astype(BF16),
        w_branch_b[0].astype(BF16), w_out[0].astype(BF16), norm_ffn[0][None],
        jnp.concatenate([wr_hi, wr_lo], axis=0), b_router[0][:, None])

    n_tiles = n // TM_DISPATCH
    p_rows = _round_up(n * TOP_K + n_tiles * N_EXPERTS * (SEG_ALIGN - 1) + N_EXPERTS * (EXPERT_BM - 1),
                       EXPERT_BM)
    s_rows = _round_up(TM_DISPATCH * TOP_K + N_EXPERTS * (SEG_ALIGN - 1), 256)
    c8, off, a, tot, tail, block_e, nused = _routing_tables(cnt[:, :, 0], p_rows)

    xs = _dispatch(c8, off, a, tot, tail, h2, pos, p_rows, s_rows)
    ys = _expert_ffn(block_e, nused, xs, w_gate_up[0], b_gate_up[0][:, None, :], w_down[0],
                     b_down[0][:, None, :])

    pos_t, gate_t = pos.T, gate.T
    outs = []
    for (b, t, g0) in groups:
        outs.append(_combine(c8, off, a, tot, ys, x1, pos_t, gate_t, norm_final[None], g0, b, t, s_rows))
    return tuple(outs)
```

```python
import functools

import numpy as np
import jax
import jax.numpy as jnp
from jax import lax
from jax.experimental import pallas as pl
from jax.experimental.pallas import tpu as pltpu

F32 = jnp.float32
BF16 = jnp.bfloat16
I32 = jnp.int32
U32 = jnp.uint32

HEAD_DIM = 64
QUAD_HEADS = 4
QUAD = QUAD_HEADS * HEAD_DIM
LANES = 128
QKV_W = 3 * QUAD
DILATED_GROUPS = ((128, 1), (512, 4), (2048, 16))
N_DIL = len(DILATED_GROUPS)
NA_QUADS = 2
N_QUADS = N_DIL + NA_QUADS
BAND_HALF = 64
T5_BUCKETS = 32
T5_MAX_DISTANCE = 1024
GRID_W = 64
NA_KH = 8
NA_KW = 16
N_EXPERTS = 32
TOP_K = 4
SWIGLU_LIMIT = 7.0
SWIGLU_ALPHA = 1.702
RMS_EPS = 1e-6
NEG = -1e30

BAND_BQ = 128
BAND_OUTER = 512
NA_BAND_ROWS = 8
TM_QKV = 512
TM_DISPATCH = 512
TM_MERGE_SUB = 256
EXPERT_BM = 1024
EXPERT_CHUNK = 256
ONEHOT_CHUNK = 256
SEG_ALIGN = 8
VMEM_LIMIT = 56 * 1024 * 1024


def _round_up(x, m):
    return (x + m - 1) // m * m


def _cparams(*sem):
    return pltpu.CompilerParams(dimension_semantics=sem, vmem_limit_bytes=VMEM_LIMIT)


def _rms(x, g):
    ms = jnp.mean(x * x, axis=-1, keepdims=True)
    return x * lax.rsqrt(ms + RMS_EPS) * g


def _select_group(w, metas, fn):
    out = fn(metas[0], w - metas[0]["base"])
    for m in metas[1:]:
        cand = fn(m, w - m["base"])
        sel = w >= m["base"]
        out = jax.tree.map(lambda a, b: jnp.where(sel, b, a), out, cand)
    return out


def _group_tile_specs(xs, tm):
    specs, bounds, tile0 = [], [], 0
    for xg in xs:
        b, t, d = xg.shape
        assert t % tm == 0
        per_seq, n_tiles = t // tm, b * t // tm

        def im(i, *_, tile0=tile0, per_seq=per_seq, n_tiles=n_tiles):
            j = jnp.clip(i - tile0, 0, n_tiles - 1)
            return (j // per_seq, j % per_seq, 0)
        specs.append(pl.BlockSpec((None, tm, d), im))
        tile0 += n_tiles
        bounds.append(tile0)
    return specs, bounds


def _group_tile(refs, bounds, rows=slice(None)):
    i = pl.program_id(0)
    x = refs[-1][rows, :]
    for ref, bound in zip(refs[-2::-1], bounds[-2::-1]):
        x = jnp.where(i < bound, ref[rows, :], x)
    return x


def _qkv_kernel(*refs, bounds):
    ng = len(bounds)
    x_refs, (g_ref, w_ref), refs = refs[:ng], refs[ng:ng + 2], refs[ng + 2:]
    band_refs, na_ref, h_ref = refs[:N_DIL], refs[N_DIL], refs[N_DIL + 1]
    n_lane_blocks, tm, _ = h_ref.shape
    h_nat = _rms(_group_tile(x_refs, bounds), g_ref[...])
    for c in range(n_lane_blocks):
        h_ref[c] = h_nat[:, c * LANES:(c + 1) * LANES]

    def project(h, j):
        return jnp.dot(h.astype(BF16), w_ref[:, j * QKV_W:(j + 1) * QKV_W],
                       preferred_element_type=F32).astype(BF16)

    def class_rows(r, rows, dil):
        return jnp.concatenate([h_ref[c, pl.ds(r, rows, stride=dil), :] for c in range(n_lane_blocks)],
                               axis=1)

    for j in range(NA_QUADS):
        na_ref[j] = project(h_nat, N_DIL + j)
    for g, (_, dil) in enumerate(DILATED_GROUPS):
        rows = tm // dil
        if dil == 1:
            band_refs[g][...] = project(h_nat, g)
            continue
        hp = jnp.concatenate([class_rows(r, rows, dil) for r in range(dil)], axis=0)
        y = project(hp, g)
        for r in range(dil):
            band_refs[g][:, r * QKV_W:(r + 1) * QKV_W] = y[r * rows:(r + 1) * rows]


def _qkv_proj(xs, g, w):
    d = xs[0].shape[-1]
    tm = TM_QKV
    x_specs, bounds = _group_tile_specs(xs, tm)
    n = bounds[-1] * tm
    band_specs = [pl.BlockSpec((tm // dil, dil * QKV_W), lambda i: (i, 0)) for _, dil in DILATED_GROUPS]
    band_shapes = [jax.ShapeDtypeStruct((n // dil, dil * QKV_W), BF16) for _, dil in DILATED_GROUPS]
    return pl.pallas_call(
        functools.partial(_qkv_kernel, bounds=bounds),
        grid=(n // tm,),
        in_specs=x_specs + [pl.BlockSpec((1, d), lambda i: (0, 0)),
                            pl.BlockSpec((d, N_QUADS * QKV_W), lambda i: (0, 0))],
        out_specs=band_specs + [pl.BlockSpec((NA_QUADS, tm, QKV_W), lambda i: (0, i, 0))],
        out_shape=band_shapes + [jax.ShapeDtypeStruct((NA_QUADS, n, QKV_W), BF16)],
        scratch_shapes=[pltpu.VMEM((d // LANES, tm, LANES), F32)],
        compiler_params=_cparams("parallel"),
    )(*xs, g, w)


def _head_lane_masks(rows):
    lane_head = lax.broadcasted_iota(I32, (rows, QUAD), 1) // HEAD_DIM
    return [lane_head == h for h in range(QUAD_HEADS)]


def _stack_heads(q, masks):
    zero = jnp.zeros_like(q)
    return jnp.concatenate([jnp.where(m, q, zero) for m in masks], axis=0)


def _band_kernel(cur_ref, prev_ref, next_ref, b_ref, o_ref, l_ref, *, decode, bq):
    bo = cur_ref.shape[0]
    nc, n_sub = cur_ref.shape[1] // QKV_W, bo // bq
    t = decode(pl.program_id(0))
    first = (t["i"] == 0).astype(I32)
    last = (t["i"] == t["last"]).astype(I32)
    masks = _head_lane_masks(bq)
    for c in range(nc):
        qcol, kcol, vcol = (slice(c * QKV_W + j * QUAD, c * QKV_W + (j + 1) * QUAD) for j in range(3))
        k = jnp.concatenate([prev_ref[:, kcol], cur_ref[:, kcol], next_ref[:, kcol]], axis=0)
        v = jnp.concatenate([prev_ref[:, vcol], cur_ref[:, vcol], next_ref[:, vcol]], axis=0)
        ocol = slice(c * QUAD, (c + 1) * QUAD)
        for j in range(n_sub):
            rows = slice(j * bq, (j + 1) * bq)
            keys = slice(j * bq, (j + 1) * bq + 2 * BAND_HALF)
            variant = (first if j == 0 else 0) + (2 * last if j == n_sub - 1 else 0)
            qs = _stack_heads(cur_ref[rows, qcol], masks)
            s = lax.dot_general(qs, k[keys], (((1,), (1,)), ((), ())), preferred_element_type=F32)
            s = s + b_ref[variant]
            m = jnp.max(s, axis=-1, keepdims=True)
            p = jnp.exp(s - m)
            l = jnp.sum(p, axis=-1, keepdims=True)
            pv = jnp.dot(p.astype(BF16), v[keys], preferred_element_type=F32) * (1.0 / l)
            lse = jnp.broadcast_to(m + jnp.log(l), pv.shape)
            o = jnp.zeros((bq, QUAD), F32)
            le = jnp.zeros((bq, QUAD), F32)
            for h in range(QUAD_HEADS):
                o = jnp.where(masks[h], pv[h * bq:(h + 1) * bq], o)
                le = jnp.where(masks[h], lse[h * bq:(h + 1) * bq], le)
            o_ref[rows, ocol] = o
            l_ref[rows, ocol] = le


def _t5_bucket(rel):
    nb = T5_BUCKETS // 2
    ret = (rel > 0).astype(np.int32) * nb
    n = np.abs(rel)
    max_exact = nb // 2
    large = max_exact + (np.log(np.maximum(n, 1) / max_exact)
                         / np.log(T5_MAX_DISTANCE / max_exact) * (nb - max_exact)).astype(np.int32)
    large = np.minimum(large, nb - 1)
    return ret + np.where(n < max_exact, n, large)


def _select_rows(table, index, n_rows):
    onehot = np.zeros((index.size, n_rows), np.float32)
    onehot[np.arange(index.size), index.reshape(-1)] = 1.0
    out = jnp.einsum("nb,b...->n...", jnp.asarray(onehot), table.astype(F32),
                     precision=lax.Precision.HIGHEST)
    return out.reshape(index.shape + table.shape[1:])


def _band_bias(table, dilation, bq):
    nk = bq + 2 * BAND_HALF
    qi = np.arange(bq)[:, None]
    c = np.arange(nk)[None, :]
    rel = c - BAND_HALF - qi
    band = np.abs(rel) <= BAND_HALF
    bias = jnp.transpose(_select_rows(table, _t5_bucket(dilation * rel), T5_BUCKETS), (2, 0, 1))
    out = []
    for variant in range(4):
        valid = band
        if variant & 1:
            valid = valid & (c >= BAND_HALF)
        if variant & 2:
            valid = valid & (c < bq + BAND_HALF)
        out.append(jnp.where(valid[None], bias, NEG).reshape(QUAD_HEADS * bq, nk))
    return jnp.stack(out, axis=0)


def _band_attention(a, dilation, bias, groups):
    d, bq = dilation, BAND_BQ
    n = a.shape[0] * d
    bo = min([BAND_OUTER] + [t // d for (_, t, _) in groups])
    nc = min(d, BAND_OUTER // bo)
    hq = bo // BAND_HALF
    metas, base = [], 0
    for (b, t, tok0) in groups:
        seq = t // d
        assert bo % bq == 0 and seq % bo == 0 and tok0 % (d * bo) == 0 and d % nc == 0
        nb = seq // bo
        metas.append(dict(base=base, nb=nb, rb0=tok0 // d // bo))
        base += b * (d // nc) * nb
    n_items = base

    def decode(w):
        def f(m, wl):
            nb = m["nb"]
            s = wl // ((d // nc) * nb)
            r = (wl // nb) % (d // nc)
            i = wl % nb
            seq0 = m["rb0"] + s * nb
            return dict(r=r, i=i, seq0=seq0, last=nb - 1 + 0 * i)
        return _select_group(w, metas, f)

    def cur_im(w):
        t = decode(w)
        return (t["seq0"] + t["i"], t["r"])

    def prev_im(w):
        t = decode(w)
        return (hq * t["seq0"] + jnp.maximum(hq * t["i"] - 1, 0), t["r"])

    def next_im(w):
        t = decode(w)
        return (hq * t["seq0"] + jnp.minimum(hq * t["i"] + hq, hq * t["last"] + hq - 1), t["r"])

    return pl.pallas_call(
        functools.partial(_band_kernel, decode=decode, bq=bq),
        grid=(n_items,),
        in_specs=[pl.BlockSpec((bo, nc * QKV_W), cur_im),
                  pl.BlockSpec((BAND_HALF, nc * QKV_W), prev_im),
                  pl.BlockSpec((BAND_HALF, nc * QKV_W), next_im),
                  pl.BlockSpec(bias.shape, lambda w: (0, 0, 0))],
        out_specs=[pl.BlockSpec((bo, nc * QUAD), cur_im), pl.BlockSpec((bo, nc * QUAD), cur_im)],
        out_shape=[jax.ShapeDtypeStruct((n // d, d * QUAD), F32)] * 2,
        compiler_params=_cparams("parallel"),
    )(a, a, a, bias)


def _na_bias(rpb):
    qc = np.arange(GRID_W)[:, None]
    kc = np.arange(GRID_W)[None, :]
    col_idx = np.clip(kc - qc + NA_KW - 1, 0, 2 * NA_KW - 2)
    ws = np.clip(qc - NA_KW // 2, 0, GRID_W - NA_KW)
    mask = (kc >= ws) & (kc < ws + NA_KW)
    t = _select_rows(jnp.moveaxis(rpb, 2, 0), col_idx, 2 * NA_KW - 1)
    t = jnp.where(mask[:, :, None, None], t, NEG)
    b = jnp.stack([t[..., dv:dv + NA_KH] for dv in range(NA_KH)], axis=0)
    b = b.reshape(NA_KH, GRID_W, GRID_W, NA_QUADS, QUAD_HEADS, NA_KH)
    b = b.transpose(3, 0, 4, 1, 5, 2)
    return b.reshape(NA_QUADS, NA_KH, QUAD_HEADS * GRID_W, NA_KH * GRID_W)


def _na_kernel(q_ref, kp_ref, kc_ref, kn_ref, vp_ref, vc_ref, vn_ref, b_ref, o_ref,
               kbuf, vbuf, *, metas, items_per_quad):
    blk = NA_BAND_ROWS * GRID_W
    w = pl.program_id(0) % items_per_quad
    t = _select_group(w, metas, lambda m, wl: dict(i=wl % m["nbands"], nbands=m["nbands"] + 0 * wl))
    band, rows = t["i"], t["nbands"] * NA_BAND_ROWS
    kbuf[0:blk] = kp_ref[...]
    kbuf[blk:2 * blk] = kc_ref[...]
    kbuf[2 * blk:3 * blk] = kn_ref[...]
    vbuf[0:blk] = vp_ref[...]
    vbuf[blk:2 * blk] = vc_ref[...]
    vbuf[2 * blk:3 * blk] = vn_ref[...]
    masks = _head_lane_masks(GRID_W)

    for j in range(NA_BAND_ROWS):
        r = band * NA_BAND_ROWS + j
        rs = jnp.clip(r - NA_KH // 2, 0, rows - NA_KH)
        loc = pl.multiple_of((rs - (band - 1) * NA_BAND_ROWS) * GRID_W, GRID_W)
        kw = kbuf[pl.ds(loc, NA_KH * GRID_W), :]
        vw = vbuf[pl.ds(loc, NA_KH * GRID_W), :]
        qrow = slice(j * GRID_W, (j + 1) * GRID_W)
        qs = _stack_heads(q_ref[qrow, :], masks)
        s = lax.dot_general(qs, kw, (((1,), (1,)), ((), ())), preferred_element_type=F32)
        s = s + b_ref[rs - r + NA_KH - 1]
        m = jnp.max(s, axis=-1, keepdims=True)
        p = jnp.exp(s - m)
        l = jnp.sum(p, axis=-1, keepdims=True)
        pv = jnp.dot(p.astype(BF16), vw, preferred_element_type=F32) * (1.0 / l)
        o = jnp.zeros((GRID_W, QUAD), F32)
        for h in range(QUAD_HEADS):
            o = jnp.where(masks[h], pv[h * GRID_W:(h + 1) * GRID_W], o)
        o_ref[qrow, :] = o


def _neighbourhood_attention(qkv, bias, groups):
    _, n, _ = qkv.shape
    blk = NA_BAND_ROWS * GRID_W
    metas, base = [], 0
    for (b, t, tok0) in groups:
        rows = t // GRID_W
        assert t % GRID_W == 0 and rows % NA_BAND_ROWS == 0 and rows >= NA_KH and tok0 % blk == 0
        nbands = rows // NA_BAND_ROWS
        metas.append(dict(base=base, nbands=nbands, tb0=tok0 // blk))
        base += b * nbands
    items_per_quad = base

    def decode(w):
        quad = w // items_per_quad
        wq = w % items_per_quad

        def f(m, wl):
            s = wl // m["nbands"]
            i = wl % m["nbands"]
            return dict(i=i, seq0=m["tb0"] + s * m["nbands"], last=m["nbands"] - 1 + 0 * i)
        t = _select_group(wq, metas, f)
        t["quad"] = quad
        return t

    def spec(col, shift):
        def im(w):
            t = decode(w)
            i = jnp.clip(t["i"] + shift, 0, t["last"])
            return (t["quad"], t["seq0"] + i, col)
        return pl.BlockSpec((None, blk, QUAD), im)

    def out_im(w):
        t = decode(w)
        return (t["seq0"] + t["i"], t["quad"])

    return pl.pallas_call(
        functools.partial(_na_kernel, metas=metas, items_per_quad=items_per_quad),
        grid=(NA_QUADS * items_per_quad,),
        in_specs=[spec(0, 0), spec(1, -1), spec(1, 0), spec(1, 1), spec(2, -1), spec(2, 0), spec(2, 1),
                  pl.BlockSpec((None, NA_KH, QUAD_HEADS * GRID_W, NA_KH * GRID_W),
                               lambda w: (w // items_per_quad, 0, 0, 0))],
        out_specs=pl.BlockSpec((blk, QUAD), out_im),
        out_shape=jax.ShapeDtypeStruct((n, NA_QUADS * QUAD), F32),
        scratch_shapes=[pltpu.VMEM((3 * blk, QUAD), BF16), pltpu.VMEM((3 * blk, QUAD), BF16)],
        compiler_params=_cparams("parallel"),
    )(qkv, qkv, qkv, qkv, qkv, qkv, qkv, bias)


def _merge_kernel(*refs, bounds):
    ng = len(bounds)
    x_refs, refs = refs[:ng], refs[ng:]
    (o0_ref, l0_ref, o1_ref, l1_ref, o2_ref, l2_ref, ob_ref,
     gmix_ref, wg_ref, wa_ref, wb_ref, wo_ref, gffn_ref, wr_ref, br_ref, tri_ref,
     x1_ref, h2_ref, gate_ref, pos_ref, cnt_ref, h2lo_ref, *nat_refs) = refs
    tm, d = x1_ref.shape
    ts = TM_MERGE_SUB

    band = [(o0_ref, l0_ref)]
    pairs = ((o1_ref, l1_ref), (o2_ref, l2_ref))
    for g, (_, dil) in enumerate(DILATED_GROUPS[1:]):
        nat = nat_refs[2 * g:2 * g + 2]
        for src, dst in zip(pairs[g], nat):
            for r in range(dil):
                for c in range(QUAD // LANES):
                    col = r * QUAD + c * LANES
                    dst[c, pl.ds(r, tm // dil, stride=dil), :] = src[:, col:col + LANES]
        band.append(nat)

    for sub in range(tm // ts):
        rows = slice(sub * ts, (sub + 1) * ts)
        x = _group_tile(x_refs, bounds, rows)
        h = _rms(x, gmix_ref[...]).astype(BF16)
        gates = jax.nn.sigmoid(jnp.dot(h, wg_ref[...], preferred_element_type=F32))

        def rows_of(ref):
            if len(ref.shape) == 2:
                return ref[rows, :]
            return jnp.concatenate([ref[c, rows, :] for c in range(ref.shape[0])], axis=1)

        (o0, l0), (o1, l1), (o2, l2) = [(rows_of(o), rows_of(l)) for o, l in band]
        m = jnp.maximum(jnp.maximum(l0, l1), l2)
        e0, e1, e2 = jnp.exp(l0 - m), jnp.exp(l1 - m), jnp.exp(l2 - m)
        oa = (e0 * o0 + e1 * o1 + e2 * o2) * (1.0 / (e0 + e1 + e2))

        bra = jnp.dot(oa.astype(BF16), wa_ref[...], preferred_element_type=F32)
        brb = jnp.dot(ob_ref[rows, :].astype(BF16), wb_ref[...], preferred_element_type=F32)
        merged = (gates[:, :d] * bra + gates[:, d:] * brb).astype(BF16)
        x1 = x + jnp.dot(merged, wo_ref[...], preferred_element_type=F32)
        x1_ref[rows, :] = x1
        h2 = _rms(x1, gffn_ref[...])
        h2_hi = h2.astype(BF16)
        h2_ref[rows, :] = h2_hi
        h2lo_ref[rows, :] = (h2 - h2_hi.astype(F32)).astype(BF16)

    nt = (((1,), (1,)), ((), ()))
    hi_terms = lax.dot_general(wr_ref[...], h2_ref[...], nt, preferred_element_type=F32)
    logits = (hi_terms[:N_EXPERTS] + hi_terms[N_EXPERTS:]
              + lax.dot_general(wr_ref[:N_EXPERTS, :], h2lo_ref[...], nt, preferred_element_type=F32)
              + br_ref[...])
    eiota = lax.broadcasted_iota(I32, (N_EXPERTS, tm), 0)
    idxs, vals = [], []
    for _ in range(TOP_K):
        top = jnp.max(logits, axis=0, keepdims=True)
        idx = jnp.min(jnp.where(logits == top, eiota, N_EXPERTS), axis=0, keepdims=True)
        idxs.append(idx)
        vals.append(top)
        logits = jnp.where(eiota == idx, -jnp.inf, logits)
    ex = [jnp.exp(v - vals[0]) for v in vals]
    inv = 1.0 / (ex[0] + ex[1] + ex[2] + ex[3])
    gate_ref[...] = jnp.concatenate([v * inv for v in ex], axis=0)

    onehots = [eiota == idx for idx in idxs]
    multi = onehots[0] | onehots[1] | onehots[2] | onehots[3]
    multi_f = jnp.where(multi, 1.0, 0.0)
    prefix = jnp.dot(multi_f.astype(BF16), tri_ref[...], preferred_element_type=F32)
    cnt = jnp.broadcast_to(jnp.sum(multi_f, axis=1, keepdims=True), cnt_ref.shape)
    cnt_ref[...] = cnt.astype(I32)
    units = jnp.floor((cnt + (SEG_ALIGN - 1)) * (1.0 / SEG_ALIGN))
    lower = (lax.broadcasted_iota(I32, (N_EXPERTS, N_EXPERTS), 1)
             < lax.broadcasted_iota(I32, (N_EXPERTS, N_EXPERTS), 0))
    seg0 = jnp.dot(jnp.where(lower, 1.0, 0.0).astype(BF16), units.astype(BF16),
                   preferred_element_type=F32)[:, 0:1] * SEG_ALIGN
    pos_ref[...] = jnp.concatenate(
        [jnp.sum(jnp.where(oh, prefix + seg0, 0.0), axis=0, keepdims=True) for oh in onehots],
        axis=0).astype(I32)


def _merge_router(xs, band, ob, gmix, wg, wa, wb, wo, gffn, wr, br):
    d = xs[0].shape[-1]
    tm = TM_DISPATCH
    x_specs, bounds = _group_tile_specs(xs, tm)
    n = bounds[-1] * tm
    tri = jnp.asarray(np.triu(np.ones((tm, tm), np.float32), k=1), BF16)

    def row(width, dil=1):
        return pl.BlockSpec((tm // dil, dil * width), lambda i: (i, 0))

    def const(shape):
        return pl.BlockSpec(shape, lambda i: (0,) * len(shape))

    def tok4():
        return pl.BlockSpec((TOP_K, tm), lambda i: (0, i))

    band_specs = [row(QUAD, dil) for _, dil in DILATED_GROUPS for _ in range(2)]
    band_args = [a for pair in band for a in pair]
    return pl.pallas_call(
        functools.partial(_merge_kernel, bounds=bounds),
        grid=(n // tm,),
        in_specs=x_specs + band_specs + [row(NA_QUADS * QUAD),
                  const((1, d)), const(wg.shape), const(wa.shape), const(wb.shape), const(wo.shape),
                  const((1, d)), const(wr.shape), const(br.shape), const(tri.shape)],
        out_specs=[row(d), row(d), tok4(), tok4(),
                   pl.BlockSpec((None, N_EXPERTS, LANES), lambda i: (i, 0, 0))],
        out_shape=[jax.ShapeDtypeStruct((n, d), F32), jax.ShapeDtypeStruct((n, d), BF16),
                   jax.ShapeDtypeStruct((TOP_K, n), F32), jax.ShapeDtypeStruct((TOP_K, n), I32),
                   jax.ShapeDtypeStruct((n // tm, N_EXPERTS, LANES), I32)],
        scratch_shapes=[pltpu.VMEM((tm, d), BF16)]
                       + [pltpu.VMEM((QUAD // LANES, tm, LANES), F32)] * (2 * (N_DIL - 1)),
        compiler_params=_cparams("parallel"),
    )(*xs, *band_args, ob, gmix, wg, wa, wb, wo, gffn, wr, br, tri)


def _pack_bf16_pair(x):
    w = x.shape[1] // 2
    lo = pltpu.bitcast(x[:, :w], U32) >> 16
    hi = pltpu.bitcast(x[:, w:], U32) & jnp.uint32(0xFFFF0000)
    return lo | hi


def _unpack_bf16_pair(u):
    lo = pltpu.bitcast(u << 16, F32).astype(BF16)
    hi = pltpu.bitcast(u & jnp.uint32(0xFFFF0000), F32).astype(BF16)
    return lo, hi


def _pow2_copies(units, stage0, hbm0, max_units, make_copy, action):
    for b in range(max_units.bit_length() - 1, -1, -1):
        size = SEG_ALIGN << b

        @pl.when(((units >> b) & 1) == 1)
        def _():
            done = ((units >> (b + 1)) << (b + 1)) * SEG_ALIGN
            cp = make_copy(pl.multiple_of(stage0 + done, SEG_ALIGN),
                           pl.multiple_of(hbm0 + done, SEG_ALIGN), size)
            if action == "start":
                cp.start()
            else:
                cp.wait()


def _segment_copies(tile, c8_ref, off_ref, a_ref, make_copy, action, enable=True):
    tile = jnp.maximum(tile, 0)
    for e in range(N_EXPERTS):
        k = tile * N_EXPERTS + e
        units = jnp.where(enable, c8_ref[k] // SEG_ALIGN, 0)
        _pow2_copies(units, off_ref[k], a_ref[k], TM_DISPATCH // SEG_ALIGN, make_copy, action)


def _segment_waits(tile, tot_ref, s_rows, make_copy, enable=True):
    units = jnp.where(enable, tot_ref[jnp.maximum(tile, 0)], 0)
    _pow2_copies(units, 0, 0, s_rows // SEG_ALIGN, make_copy, "wait")


def _dispatch_kernel(c8_ref, off_ref, a_ref, tot_ref, tail_ref, h2_ref, pos_ref, xs_ref, stage_ref, sem):
    tile = pl.program_id(0)
    last = pl.num_programs(0) - 1
    slot = tile % 2
    s_rows, tm = stage_ref.shape[1], h2_ref.shape[0]

    def copier(buf):
        def make_copy(stage_row, hbm_row, size):
            return pltpu.make_async_copy(stage_ref.at[buf, pl.ds(stage_row, size)],
                                         xs_ref.at[pl.ds(hbm_row, size)], sem.at[buf])
        return make_copy

    @pl.when(tile == 0)
    def _():
        stage_ref[0, 0:EXPERT_BM] = jnp.zeros((EXPERT_BM, stage_ref.shape[2]), U32)
        for action in ("start", "wait"):
            for e in range(N_EXPERTS):
                _pow2_copies(tail_ref[N_EXPERTS + e], 0, tail_ref[e], EXPERT_BM // SEG_ALIGN - 1,
                             copier(0), action)

        def zero_block(b, carry):
            cp = copier(0)(0, pl.multiple_of(b * EXPERT_BM, EXPERT_BM), EXPERT_BM)
            cp.start()
            cp.wait()
            return carry
        lax.fori_loop(tail_ref[2 * N_EXPERTS], xs_ref.shape[0] // EXPERT_BM, zero_block, 0)

    _segment_waits(tile - 2, tot_ref, s_rows, copier(slot), enable=tile >= 2)
    _segment_copies(tile - 1, c8_ref, off_ref, a_ref, copier(1 - slot), "start", enable=tile >= 1)

    pos = pos_ref[...]
    h2 = h2_ref[...]
    ck = ONEHOT_CHUNK
    for c in range(s_rows // ck):
        siota = lax.broadcasted_iota(I32, (ck, tm), 0) + c * ck
        compact = jnp.zeros((ck, tm), F32)
        for k in range(TOP_K):
            compact = jnp.where(siota == pos[k:k + 1, :], 1.0, compact)
        rows = jnp.dot(compact.astype(BF16), h2, preferred_element_type=F32)
        stage_ref[slot, c * ck:(c + 1) * ck, :] = _pack_bf16_pair(rows)

    @pl.when(tile == last)
    def _():
        _segment_copies(tile, c8_ref, off_ref, a_ref, copier(slot), "start")
        _segment_waits(tile - 1, tot_ref, s_rows, copier(1 - slot), enable=tile >= 1)
        _segment_waits(tile, tot_ref, s_rows, copier(slot))


def _dispatch(c8, off, a, tot, tail, h2, pos, p_rows, s_rows):
    n, d = h2.shape
    tm = TM_DISPATCH
    grid_spec = pltpu.PrefetchScalarGridSpec(
        num_scalar_prefetch=5,
        grid=(n // tm,),
        in_specs=[pl.BlockSpec((tm, d), lambda i, *_: (i, 0)),
                  pl.BlockSpec((TOP_K, tm), lambda i, *_: (0, i))],
        out_specs=pl.BlockSpec(memory_space=pl.ANY),
        scratch_shapes=[pltpu.VMEM((2, s_rows, d // 2), U32), pltpu.SemaphoreType.DMA((2,))],
    )
    return pl.pallas_call(
        _dispatch_kernel,
        grid_spec=grid_spec,
        out_shape=jax.ShapeDtypeStruct((p_rows, d // 2), U32),
        compiler_params=_cparams("arbitrary"),
    )(c8, off, a, tot, tail, h2, pos)


def _expert_kernel(be_ref, nused_ref, xs_ref, wgu_ref, bgu_ref, wd_ref, bd_ref, ys_ref, wgu_bf, wd_bf):
    b = pl.program_id(0)
    active = b < nused_ref[0]
    new_expert = (b == 0) | (be_ref[b] != be_ref[jnp.maximum(b - 1, 0)])

    @pl.when(jnp.logical_not(active))
    def _():
        ys_ref[...] = jnp.zeros_like(ys_ref)

    @pl.when(active & new_expert)
    def _():
        wgu_bf[...] = wgu_ref[...].astype(BF16)
        wd_bf[...] = wd_ref[...].astype(BF16)

    @pl.when(active)
    def _():
        lo, hi = _unpack_bf16_pair(xs_ref[...])
        x = jnp.concatenate([lo, hi], axis=1)
        ck = EXPERT_CHUNK
        de = wd_bf.shape[0]
        y = None
        for c in range(de // ck):
            glu_cols, lin_cols = slice(c * ck, (c + 1) * ck), slice(de + c * ck, de + (c + 1) * ck)
            g = jnp.dot(x, wgu_bf[:, glu_cols], preferred_element_type=F32) + bgu_ref[:, glu_cols]
            u = jnp.dot(x, wgu_bf[:, lin_cols], preferred_element_type=F32) + bgu_ref[:, lin_cols]
            x_glu = jnp.minimum(g, SWIGLU_LIMIT)
            x_lin = jnp.clip(u, -SWIGLU_LIMIT, SWIGLU_LIMIT)
            act = x_glu * jax.nn.sigmoid(SWIGLU_ALPHA * x_glu) * (x_lin + 1.0)
            part = jnp.dot(act.astype(BF16), wd_bf[c * ck:(c + 1) * ck, :], preferred_element_type=F32)
            y = part if y is None else y + part
        y = y + bd_ref[...]
        ys_ref[...] = _pack_bf16_pair(y.astype(BF16).astype(F32))


def _expert_ffn(block_e, nused, xs, wgu, bgu, wd, bd):
    p_rows, half = xs.shape
    bm = EXPERT_BM
    d, de2 = wgu.shape[1], wgu.shape[2]

    def blk(b, be, nu):
        return (jnp.minimum(b, nu[0] - 1), 0)

    def wsel(b, be, nu):
        return (be[jnp.minimum(b, nu[0] - 1)], 0, 0)

    grid_spec = pltpu.PrefetchScalarGridSpec(
        num_scalar_prefetch=2,
        grid=(p_rows // bm,),
        in_specs=[pl.BlockSpec((bm, half), blk),
                  pl.BlockSpec((None, d, de2), wsel),
                  pl.BlockSpec((None, 1, de2), wsel),
                  pl.BlockSpec((None, de2 // 2, d), wsel),
                  pl.BlockSpec((None, 1, d), wsel)],
        out_specs=pl.BlockSpec((bm, half), lambda b, be, nu: (b, 0)),
        scratch_shapes=[pltpu.VMEM((d, de2), BF16), pltpu.VMEM((de2 // 2, d), BF16)],
    )
    return pl.pallas_call(
        _expert_kernel,
        grid_spec=grid_spec,
        out_shape=jax.ShapeDtypeStruct((p_rows, half), U32),
        compiler_params=_cparams("arbitrary"),
    )(block_e, nused, xs, wgu, bgu, wd, bd)


def _combine_kernel(c8_ref, off_ref, a_ref, tot_ref, ys_ref, x1_ref, pos_ref, gate_ref, g_ref, o_ref,
                    stage0_ref, stage1_ref, sem, *, tile0):
    step = pl.program_id(0)
    tile = step + tile0
    stages = (stage0_ref, stage1_ref)
    s_rows, tm = stage0_ref.shape[0], x1_ref.shape[0]

    def copier(buf):
        def make_copy(stage_row, hbm_row, size):
            return pltpu.make_async_copy(ys_ref.at[pl.ds(hbm_row, size)],
                                         stages[buf].at[pl.ds(stage_row, size)], sem.at[buf])
        return make_copy

    @pl.when(step == 0)
    def _():
        for st in stages:
            st[...] = jnp.zeros_like(st)
        _segment_copies(tile, c8_ref, off_ref, a_ref, copier(0), "start")

    def tile_body(cur):
        _segment_waits(tile, tot_ref, s_rows, copier(cur))
        next_tile = jnp.minimum(tile + 1, tot_ref.shape[0] - 1)
        _segment_copies(next_tile, c8_ref, off_ref, a_ref, copier(1 - cur), "start",
                        enable=step + 1 < pl.num_programs(0))
        pos = pos_ref[...]
        gate = gate_ref[...]
        ck = ONEHOT_CHUNK
        moe_lo = moe_hi = None
        for c in range(s_rows // ck):
            liota = lax.broadcasted_iota(I32, (tm, ck), 1) + c * ck
            gmat = jnp.zeros((tm, ck), F32)
            for k in range(TOP_K):
                gmat = jnp.where(liota == pos[:, k:k + 1], gate[:, k:k + 1], gmat)
            gmat = gmat.astype(BF16)
            lo, hi = _unpack_bf16_pair(stages[cur][c * ck:(c + 1) * ck, :])
            part_lo = jnp.dot(gmat, lo, preferred_element_type=F32)
            part_hi = jnp.dot(gmat, hi, preferred_element_type=F32)
            moe_lo = part_lo if moe_lo is None else moe_lo + part_lo
            moe_hi = part_hi if moe_hi is None else moe_hi + part_hi
        moe = jnp.concatenate([moe_lo, moe_hi], axis=1)
        o_ref[...] = _rms(x1_ref[...] + moe, g_ref[...])

    for cur in range(2):
        pl.when(step % 2 == cur)(functools.partial(tile_body, cur))


def _combine(c8, off, a, tot, ys, x1, pos_t, gate_t, gfinal, tok0, batch, seq, s_rows):
    _, d = x1.shape
    tm = TM_DISPATCH
    tile0, per_seq = tok0 // tm, seq // tm

    def row(width):
        return pl.BlockSpec((tm, width), lambda i, *_: (i + tile0, 0))

    grid_spec = pltpu.PrefetchScalarGridSpec(
        num_scalar_prefetch=4,
        grid=(batch * per_seq,),
        in_specs=[pl.BlockSpec(memory_space=pl.ANY), row(d), row(TOP_K), row(TOP_K),
                  pl.BlockSpec((1, d), lambda i, *_: (0, 0))],
        out_specs=pl.BlockSpec((None, tm, d), lambda i, *_: (i // per_seq, i % per_seq, 0)),
        scratch_shapes=[pltpu.VMEM((s_rows, d // 2), U32)] * 2 + [pltpu.SemaphoreType.DMA((2,))],
    )
    return pl.pallas_call(
        functools.partial(_combine_kernel, tile0=tile0),
        grid_spec=grid_spec,
        out_shape=jax.ShapeDtypeStruct((batch, seq, d), F32),
        compiler_params=_cparams("arbitrary"),
    )(c8, off, a, tot, ys, x1, pos_t, gate_t, gfinal)


def _routing_tables(cnt, p_rows):
    c8 = _round_up(cnt, SEG_ALIGN)
    off = jnp.cumsum(c8, axis=1) - c8
    tot = jnp.sum(c8, axis=0)
    padded = _round_up(tot, EXPERT_BM)
    pend = jnp.cumsum(padded)
    a = (pend - padded)[None, :] + jnp.cumsum(c8, axis=0) - c8
    n_blocks = p_rows // EXPERT_BM
    block_row = jnp.arange(n_blocks, dtype=I32) * EXPERT_BM
    block_e = jnp.sum((block_row[:, None] >= pend[None, :]).astype(I32), axis=1)
    block_e = jnp.minimum(block_e, N_EXPERTS - 1).astype(I32)
    nused = (pend[-1] // EXPERT_BM).astype(I32).reshape(1)
    tail = jnp.concatenate([pend - padded + tot, (padded - tot) // SEG_ALIGN, nused]).astype(I32)
    tile_units = (jnp.sum(c8, axis=1) // SEG_ALIGN).astype(I32)
    flat = lambda t: t.reshape(-1).astype(I32)
    return flat(c8), flat(off), flat(a), tile_units, tail, block_e, nused


def _permute_qkv_weight(w_in):
    d = w_in.shape[0]
    w = w_in.reshape(d, 3, N_QUADS, QUAD)
    scale = jnp.asarray([HEAD_DIM ** -0.5, 1.0, 1.0], F32).reshape(1, 3, 1, 1)
    return (w * scale).transpose(0, 2, 1, 3).reshape(d, N_QUADS * QKV_W).astype(BF16)


def kernel(x_prompt, x_sample, t5_rel_bias, norm_mix, w_in, na_rpb, w_branch_a, w_branch_b, w_gate, w_out,
           norm_ffn, w_router, b_router, w_gate_up, b_gate_up, w_down, b_down, norm_final):
    assert norm_mix.shape[0] == 1, "single-layer encoder"
    d = x_prompt.shape[-1]
    groups, tok0 = [], 0
    for xg in (x_prompt, x_sample):
        groups.append((xg.shape[0], xg.shape[1], tok0))
        tok0 += xg.shape[0] * xg.shape[1]
    n = tok0
    assert n % TM_QKV == 0 and n % TM_DISPATCH == 0 and all(g[2] % TM_DISPATCH == 0 for g in groups)
    xs_in = (x_prompt, x_sample)

    *qkv_band, qkv_na = _qkv_proj(xs_in, norm_mix[0][None], _permute_qkv_weight(w_in[0]))

    band = []
    for g, (window, dilation) in enumerate(DILATED_GROUPS):
        assert window // (2 * dilation) == BAND_HALF
        bias = _band_bias(t5_rel_bias[:, g * QUAD_HEADS:(g + 1) * QUAD_HEADS], dilation, BAND_BQ)
        band.append(_band_attention(qkv_band[g], dilation, bias, groups))
    ob = _neighbourhood_attention(qkv_na, _na_bias(na_rpb[0]), groups)

    wr = w_router[0].T
    wr_hi = wr.astype(BF16)
    wr_lo = (wr - wr_hi.astype(F32)).astype(BF16)
    x1, h2, gate, pos, cnt = _merge_router(
        xs_in, band, ob, norm_mix[0][None], w_gate[0].astype(BF16), w_branch_a[0].astype(BF16),
        w_branch_b[0].astype(BF16), w_out[0].astype(BF16), norm_ffn[0][None],
        jnp.concatenate([wr_hi, wr_lo], axis=0), b_router[0][:, None])

    n_tiles = n // TM_DISPATCH
    p_rows = _round_up(n * TOP_K + n_tiles * N_EXPERTS * (SEG_ALIGN - 1) + N_EXPERTS * (EXPERT_BM - 1),
                       EXPERT_BM)
    s_rows = _round_up(TM_DISPATCH * TOP_K + N_EXPERTS * (SEG_ALIGN - 1), 256)
    c8, off, a, tot, tail, block_e, nused = _routing_tables(cnt[:, :, 0], p_rows)

    xs = _dispatch(c8, off, a, tot, tail, h2, pos, p_rows, s_rows)
    ys = _expert_ffn(block_e, nused, xs, w_gate_up[0], b_gate_up[0][:, None, :], w_down[0],
                     b_down[0][:, None, :])

    pos_t, gate_t = pos.T, gate.T
    outs = []
    for (b, t, g0) in groups:
        outs.append(_combine(c8, off, a, tot, ys, x1, pos_t, gate_t, norm_final[None], g0, b, t, s_rows))
    return tuple(outs)
```

```python
import functools

import numpy as np
import jax
import jax.numpy as jnp
from jax import lax
from jax.experimental import pallas as pl
from jax.experimental.pallas import tpu as pltpu

F32 = jnp.float32
BF16 = jnp.bfloat16
I32 = jnp.int32
U32 = jnp.uint32

HEAD_DIM = 64
QUAD_HEADS = 4
QUAD = QUAD_HEADS * HEAD_DIM
LANES = 128
QKV_W = 3 * QUAD
DILATED_GROUPS = ((128, 1), (512, 4), (2048, 16))
N_DIL = len(DILATED_GROUPS)
NA_QUADS = 2
N_QUADS = N_DIL + NA_QUADS
BAND_HALF = 64
T5_BUCKETS = 32
T5_MAX_DISTANCE = 1024
GRID_W = 64
NA_KH = 8
NA_KW = 16
N_EXPERTS = 32
TOP_K = 4
SWIGLU_LIMIT = 7.0
SWIGLU_ALPHA = 1.702
RMS_EPS = 1e-6
NEG = -1e30

BAND_BQ = 128
BAND_OUTER = 512
NA_BAND_ROWS = 8
TM_QKV = 512
TM_DISPATCH = 512
TM_MERGE_SUB = 256
EXPERT_BM = 1024
EXPERT_CHUNK = 256
EXPERT_ROW_SLICES = 2
ONEHOT_CHUNK = 256
SEG_ALIGN = 8
VMEM_LIMIT = 56 * 1024 * 1024


def _round_up(x, m):
    return (x + m - 1) // m * m


def _cparams(*sem):
    return pltpu.CompilerParams(dimension_semantics=sem, vmem_limit_bytes=VMEM_LIMIT)


def _rms(x, g):
    ms = jnp.mean(x * x, axis=-1, keepdims=True)
    return x * lax.rsqrt(ms + RMS_EPS) * g


def _select_group(w, metas, fn):
    out = fn(metas[0], w - metas[0]["base"])
    for m in metas[1:]:
        cand = fn(m, w - m["base"])
        sel = w >= m["base"]
        out = jax.tree.map(lambda a, b: jnp.where(sel, b, a), out, cand)
    return out


def _group_tile_specs(xs, tm):
    specs, bounds, tile0 = [], [], 0
    for xg in xs:
        b, t, d = xg.shape
        assert t % tm == 0
        per_seq, n_tiles = t // tm, b * t // tm

        def im(i, *_, tile0=tile0, per_seq=per_seq, n_tiles=n_tiles):
            j = jnp.clip(i - tile0, 0, n_tiles - 1)
            return (j // per_seq, j % per_seq, 0)
        specs.append(pl.BlockSpec((None, tm, d), im))
        tile0 += n_tiles
        bounds.append(tile0)
    return specs, bounds


def _group_tile(refs, bounds, rows=slice(None)):
    i = pl.program_id(0)
    x = refs[-1][rows, :]
    for ref, bound in zip(refs[-2::-1], bounds[-2::-1]):
        x = jnp.where(i < bound, ref[rows, :], x)
    return x


def _qkv_kernel(*refs, bounds):
    ng = len(bounds)
    x_refs, (g_ref, w_ref), refs = refs[:ng], refs[ng:ng + 2], refs[ng + 2:]
    band_refs, na_ref, h_ref = refs[:N_DIL], refs[N_DIL], refs[N_DIL + 1]
    n_lane_blocks, tm, _ = h_ref.shape
    h_nat = _rms(_group_tile(x_refs, bounds), g_ref[...])
    for c in range(n_lane_blocks):
        h_ref[c] = h_nat[:, c * LANES:(c + 1) * LANES]

    def project(h, j):
        return jnp.dot(h.astype(BF16), w_ref[:, j * QKV_W:(j + 1) * QKV_W],
                       preferred_element_type=F32).astype(BF16)

    def class_rows(r, rows, dil):
        return jnp.concatenate([h_ref[c, pl.ds(r, rows, stride=dil), :] for c in range(n_lane_blocks)],
                               axis=1)

    for j in range(NA_QUADS):
        na_ref[j] = project(h_nat, N_DIL + j)
    for g, (_, dil) in enumerate(DILATED_GROUPS):
        rows = tm // dil
        if dil == 1:
            band_refs[g][...] = project(h_nat, g)
            continue
        hp = jnp.concatenate([class_rows(r, rows, dil) for r in range(dil)], axis=0)
        y = project(hp, g)
        for r in range(dil):
            band_refs[g][:, r * QKV_W:(r + 1) * QKV_W] = y[r * rows:(r + 1) * rows]


def _qkv_proj(xs, g, w):
    d = xs[0].shape[-1]
    tm = TM_QKV
    x_specs, bounds = _group_tile_specs(xs, tm)
    n = bounds[-1] * tm
    band_specs = [pl.BlockSpec((tm // dil, dil * QKV_W), lambda i: (i, 0)) for _, dil in DILATED_GROUPS]
    band_shapes = [jax.ShapeDtypeStruct((n // dil, dil * QKV_W), BF16) for _, dil in DILATED_GROUPS]
    return pl.pallas_call(
        functools.partial(_qkv_kernel, bounds=bounds),
        grid=(n // tm,),
        in_specs=x_specs + [pl.BlockSpec((1, d), lambda i: (0, 0)),
                            pl.BlockSpec((d, N_QUADS * QKV_W), lambda i: (0, 0))],
        out_specs=band_specs + [pl.BlockSpec((NA_QUADS, tm, QKV_W), lambda i: (0, i, 0))],
        out_shape=band_shapes + [jax.ShapeDtypeStruct((NA_QUADS, n, QKV_W), BF16)],
        scratch_shapes=[pltpu.VMEM((d // LANES, tm, LANES), F32)],
        compiler_params=_cparams("parallel"),
    )(*xs, g, w)


def _head_lane_masks(rows):
    lane_head = lax.broadcasted_iota(I32, (rows, QUAD), 1) // HEAD_DIM
    return [lane_head == h for h in range(QUAD_HEADS)]


def _stack_heads(q, masks):
    zero = jnp.zeros_like(q)
    return jnp.concatenate([jnp.where(m, q, zero) for m in masks], axis=0)


def _band_kernel(cur_ref, prev_ref, next_ref, b_ref, o_ref, l_ref, *, decode, bq):
    bo = cur_ref.shape[0]
    nc, n_sub = cur_ref.shape[1] // QKV_W, bo // bq
    t = decode(pl.program_id(0))
    first = (t["i"] == 0).astype(I32)
    last = (t["i"] == t["last"]).astype(I32)
    masks = _head_lane_masks(bq)
    for c in range(nc):
        qcol, kcol, vcol = (slice(c * QKV_W + j * QUAD, c * QKV_W + (j + 1) * QUAD) for j in range(3))
        k = jnp.concatenate([prev_ref[:, kcol], cur_ref[:, kcol], next_ref[:, kcol]], axis=0)
        v = jnp.concatenate([prev_ref[:, vcol], cur_ref[:, vcol], next_ref[:, vcol]], axis=0)
        ocol = slice(c * QUAD, (c + 1) * QUAD)
        for j in range(n_sub):
            rows = slice(j * bq, (j + 1) * bq)
            keys = slice(j * bq, (j + 1) * bq + 2 * BAND_HALF)
            variant = (first if j == 0 else 0) + (2 * last if j == n_sub - 1 else 0)
            qs = _stack_heads(cur_ref[rows, qcol], masks)
            s = lax.dot_general(qs, k[keys], (((1,), (1,)), ((), ())), preferred_element_type=F32)
            s = s + b_ref[variant]
            m = jnp.max(s, axis=-1, keepdims=True)
            p = jnp.exp(s - m)
            l = jnp.sum(p, axis=-1, keepdims=True)
            pv = jnp.dot(p.astype(BF16), v[keys], preferred_element_type=F32) * (1.0 / l)
            lse = jnp.broadcast_to(m + jnp.log(l), pv.shape)
            o = jnp.zeros((bq, QUAD), F32)
            le = jnp.zeros((bq, QUAD), F32)
            for h in range(QUAD_HEADS):
                o = jnp.where(masks[h], pv[h * bq:(h + 1) * bq], o)
                le = jnp.where(masks[h], lse[h * bq:(h + 1) * bq], le)
            o_ref[rows, ocol] = o
            l_ref[rows, ocol] = le


def _t5_bucket(rel):
    nb = T5_BUCKETS // 2
    ret = (rel > 0).astype(np.int32) * nb
    n = np.abs(rel)
    max_exact = nb // 2
    large = max_exact + (np.log(np.maximum(n, 1) / max_exact)
                         / np.log(T5_MAX_DISTANCE / max_exact) * (nb - max_exact)).astype(np.int32)
    large = np.minimum(large, nb - 1)
    return ret + np.where(n < max_exact, n, large)


def _select_rows(table, index, n_rows):
    onehot = np.zeros((index.size, n_rows), np.float32)
    onehot[np.arange(index.size), index.reshape(-1)] = 1.0
    out = jnp.einsum("nb,b...->n...", jnp.asarray(onehot), table.astype(F32),
                     precision=lax.Precision.HIGHEST)
    return out.reshape(index.shape + table.shape[1:])


def _band_bias(table, dilation, bq):
    nk = bq + 2 * BAND_HALF
    qi = np.arange(bq)[:, None]
    c = np.arange(nk)[None, :]
    rel = c - BAND_HALF - qi
    band = np.abs(rel) <= BAND_HALF
    bias = jnp.transpose(_select_rows(table, _t5_bucket(dilation * rel), T5_BUCKETS), (2, 0, 1))
    out = []
    for variant in range(4):
        valid = band
        if variant & 1:
            valid = valid & (c >= BAND_HALF)
        if variant & 2:
            valid = valid & (c < bq + BAND_HALF)
        out.append(jnp.where(valid[None], bias, NEG).reshape(QUAD_HEADS * bq, nk))
    return jnp.stack(out, axis=0)


def _band_attention(a, dilation, bias, groups):
    d, bq = dilation, BAND_BQ
    n = a.shape[0] * d
    bo = min([BAND_OUTER] + [t // d for (_, t, _) in groups])
    nc = min(d, BAND_OUTER // bo)
    hq = bo // BAND_HALF
    metas, base = [], 0
    for (b, t, tok0) in groups:
        seq = t // d
        assert bo % bq == 0 and seq % bo == 0 and tok0 % (d * bo) == 0 and d % nc == 0
        nb = seq // bo
        metas.append(dict(base=base, nb=nb, rb0=tok0 // d // bo))
        base += b * (d // nc) * nb
    n_items = base

    def decode(w):
        def f(m, wl):
            nb = m["nb"]
            s = wl // ((d // nc) * nb)
            r = (wl // nb) % (d // nc)
            i = wl % nb
            seq0 = m["rb0"] + s * nb
            return dict(r=r, i=i, seq0=seq0, last=nb - 1 + 0 * i)
        return _select_group(w, metas, f)

    def cur_im(w):
        t = decode(w)
        return (t["seq0"] + t["i"], t["r"])

    def prev_im(w):
        t = decode(w)
        return (hq * t["seq0"] + jnp.maximum(hq * t["i"] - 1, 0), t["r"])

    def next_im(w):
        t = decode(w)
        return (hq * t["seq0"] + jnp.minimum(hq * t["i"] + hq, hq * t["last"] + hq - 1), t["r"])

    return pl.pallas_call(
        functools.partial(_band_kernel, decode=decode, bq=bq),
        grid=(n_items,),
        in_specs=[pl.BlockSpec((bo, nc * QKV_W), cur_im),
                  pl.BlockSpec((BAND_HALF, nc * QKV_W), prev_im),
                  pl.BlockSpec((BAND_HALF, nc * QKV_W), next_im),
                  pl.BlockSpec(bias.shape, lambda w: (0, 0, 0))],
        out_specs=[pl.BlockSpec((bo, nc * QUAD), cur_im), pl.BlockSpec((bo, nc * QUAD), cur_im)],
        out_shape=[jax.ShapeDtypeStruct((n // d, d * QUAD), F32)] * 2,
        compiler_params=_cparams("parallel"),
    )(a, a, a, bias)


def _na_bias(rpb):
    qc = np.arange(GRID_W)[:, None]
    kc = np.arange(GRID_W)[None, :]
    col_idx = np.clip(kc - qc + NA_KW - 1, 0, 2 * NA_KW - 2)
    ws = np.clip(qc - NA_KW // 2, 0, GRID_W - NA_KW)
    mask = (kc >= ws) & (kc < ws + NA_KW)
    t = _select_rows(jnp.moveaxis(rpb, 2, 0), col_idx, 2 * NA_KW - 1)
    t = jnp.where(mask[:, :, None, None], t, NEG)
    b = jnp.stack([t[..., dv:dv + NA_KH] for dv in range(NA_KH)], axis=0)
    b = b.reshape(NA_KH, GRID_W, GRID_W, NA_QUADS, QUAD_HEADS, NA_KH)
    b = b.transpose(3, 0, 4, 1, 5, 2)
    return b.reshape(NA_QUADS, NA_KH, QUAD_HEADS * GRID_W, NA_KH * GRID_W)


def _na_kernel(q_ref, kp_ref, kc_ref, kn_ref, vp_ref, vc_ref, vn_ref, b_ref, o_ref,
               kbuf, vbuf, *, metas, items_per_quad):
    blk = NA_BAND_ROWS * GRID_W
    w = pl.program_id(0) % items_per_quad
    t = _select_group(w, metas, lambda m, wl: dict(i=wl % m["nbands"], nbands=m["nbands"] + 0 * wl))
    band, rows = t["i"], t["nbands"] * NA_BAND_ROWS
    kbuf[0:blk] = kp_ref[...]
    kbuf[blk:2 * blk] = kc_ref[...]
    kbuf[2 * blk:3 * blk] = kn_ref[...]
    vbuf[0:blk] = vp_ref[...]
    vbuf[blk:2 * blk] = vc_ref[...]
    vbuf[2 * blk:3 * blk] = vn_ref[...]
    masks = _head_lane_masks(GRID_W)

    for j in range(NA_BAND_ROWS):
        r = band * NA_BAND_ROWS + j
        rs = jnp.clip(r - NA_KH // 2, 0, rows - NA_KH)
        loc = pl.multiple_of((rs - (band - 1) * NA_BAND_ROWS) * GRID_W, GRID_W)
        kw = kbuf[pl.ds(loc, NA_KH * GRID_W), :]
        vw = vbuf[pl.ds(loc, NA_KH * GRID_W), :]
        qrow = slice(j * GRID_W, (j + 1) * GRID_W)
        qs = _stack_heads(q_ref[qrow, :], masks)
        s = lax.dot_general(qs, kw, (((1,), (1,)), ((), ())), preferred_element_type=F32)
        s = s + b_ref[rs - r + NA_KH - 1]
        m = jnp.max(s, axis=-1, keepdims=True)
        p = jnp.exp(s - m)
        l = jnp.sum(p, axis=-1, keepdims=True)
        pv = jnp.dot(p.astype(BF16), vw, preferred_element_type=F32) * (1.0 / l)
        o = jnp.zeros((GRID_W, QUAD), F32)
        for h in range(QUAD_HEADS):
            o = jnp.where(masks[h], pv[h * GRID_W:(h + 1) * GRID_W], o)
        o_ref[qrow, :] = o


def _neighbourhood_attention(qkv, bias, groups):
    _, n, _ = qkv.shape
    blk = NA_BAND_ROWS * GRID_W
    metas, base = [], 0
    for (b, t, tok0) in groups:
        rows = t // GRID_W
        assert t % GRID_W == 0 and rows % NA_BAND_ROWS == 0 and rows >= NA_KH and tok0 % blk == 0
        nbands = rows // NA_BAND_ROWS
        metas.append(dict(base=base, nbands=nbands, tb0=tok0 // blk))
        base += b * nbands
    items_per_quad = base

    def decode(w):
        quad = w // items_per_quad
        wq = w % items_per_quad

        def f(m, wl):
            s = wl // m["nbands"]
            i = wl % m["nbands"]
            return dict(i=i, seq0=m["tb0"] + s * m["nbands"], last=m["nbands"] - 1 + 0 * i)
        t = _select_group(wq, metas, f)
        t["quad"] = quad
        return t

    def spec(col, shift):
        def im(w):
            t = decode(w)
            i = jnp.clip(t["i"] + shift, 0, t["last"])
            return (t["quad"], t["seq0"] + i, col)
        return pl.BlockSpec((None, blk, QUAD), im)

    def out_im(w):
        t = decode(w)
        return (t["seq0"] + t["i"], t["quad"])

    return pl.pallas_call(
        functools.partial(_na_kernel, metas=metas, items_per_quad=items_per_quad),
        grid=(NA_QUADS * items_per_quad,),
        in_specs=[spec(0, 0), spec(1, -1), spec(1, 0), spec(1, 1), spec(2, -1), spec(2, 0), spec(2, 1),
                  pl.BlockSpec((None, NA_KH, QUAD_HEADS * GRID_W, NA_KH * GRID_W),
                               lambda w: (w // items_per_quad, 0, 0, 0))],
        out_specs=pl.BlockSpec((blk, QUAD), out_im),
        out_shape=jax.ShapeDtypeStruct((n, NA_QUADS * QUAD), F32),
        scratch_shapes=[pltpu.VMEM((3 * blk, QUAD), BF16), pltpu.VMEM((3 * blk, QUAD), BF16)],
        compiler_params=_cparams("parallel"),
    )(qkv, qkv, qkv, qkv, qkv, qkv, qkv, bias)


def _merge_kernel(*refs, bounds):
    ng = len(bounds)
    x_refs, refs = refs[:ng], refs[ng:]
    (o0_ref, l0_ref, o1_ref, l1_ref, o2_ref, l2_ref, ob_ref,
     gmix_ref, wg_ref, wa_ref, wb_ref, wo_ref, gffn_ref, wr_ref, br_ref, tri_ref,
     x1_ref, h2_ref, gate_ref, pos_ref, cnt_ref, h2lo_ref, *nat_refs) = refs
    tm, d = x1_ref.shape
    ts = TM_MERGE_SUB

    band = [(o0_ref, l0_ref)]
    pairs = ((o1_ref, l1_ref), (o2_ref, l2_ref))
    for g, (_, dil) in enumerate(DILATED_GROUPS[1:]):
        nat = nat_refs[2 * g:2 * g + 2]
        for src, dst in zip(pairs[g], nat):
            for r in range(dil):
                for c in range(QUAD // LANES):
                    col = r * QUAD + c * LANES
                    dst[c, pl.ds(r, tm // dil, stride=dil), :] = src[:, col:col + LANES]
        band.append(nat)

    for sub in range(tm // ts):
        rows = slice(sub * ts, (sub + 1) * ts)
        x = _group_tile(x_refs, bounds, rows)
        h = _rms(x, gmix_ref[...]).astype(BF16)
        gates = jax.nn.sigmoid(jnp.dot(h, wg_ref[...], preferred_element_type=F32))

        def rows_of(ref):
            if len(ref.shape) == 2:
                return ref[rows, :]
            return jnp.concatenate([ref[c, rows, :] for c in range(ref.shape[0])], axis=1)

        (o0, l0), (o1, l1), (o2, l2) = [(rows_of(o), rows_of(l)) for o, l in band]
        m = jnp.maximum(jnp.maximum(l0, l1), l2)
        e0, e1, e2 = jnp.exp(l0 - m), jnp.exp(l1 - m), jnp.exp(l2 - m)
        oa = (e0 * o0 + e1 * o1 + e2 * o2) * (1.0 / (e0 + e1 + e2))

        bra = jnp.dot(oa.astype(BF16), wa_ref[...], preferred_element_type=F32)
        brb = jnp.dot(ob_ref[rows, :].astype(BF16), wb_ref[...], preferred_element_type=F32)
        merged = (gates[:, :d] * bra + gates[:, d:] * brb).astype(BF16)
        x1 = x + jnp.dot(merged, wo_ref[...], preferred_element_type=F32)
        x1_ref[rows, :] = x1
        h2 = _rms(x1, gffn_ref[...])
        h2_hi = h2.astype(BF16)
        h2_ref[rows, :] = h2_hi
        h2lo_ref[rows, :] = (h2 - h2_hi.astype(F32)).astype(BF16)

    nt = (((1,), (1,)), ((), ()))
    hi_terms = lax.dot_general(wr_ref[...], h2_ref[...], nt, preferred_element_type=F32)
    logits = (hi_terms[:N_EXPERTS] + hi_terms[N_EXPERTS:]
              + lax.dot_general(wr_ref[:N_EXPERTS, :], h2lo_ref[...], nt, preferred_element_type=F32)
              + br_ref[...])
    eiota = lax.broadcasted_iota(I32, (N_EXPERTS, tm), 0)
    idxs, vals = [], []
    for _ in range(TOP_K):
        top = jnp.max(logits, axis=0, keepdims=True)
        idx = jnp.min(jnp.where(logits == top, eiota, N_EXPERTS), axis=0, keepdims=True)
        idxs.append(idx)
        vals.append(top)
        logits = jnp.where(eiota == idx, -jnp.inf, logits)
    ex = [jnp.exp(v - vals[0]) for v in vals]
    inv = 1.0 / (ex[0] + ex[1] + ex[2] + ex[3])
    gate_ref[...] = jnp.concatenate([v * inv for v in ex], axis=0)

    onehots = [eiota == idx for idx in idxs]
    multi = onehots[0] | onehots[1] | onehots[2] | onehots[3]
    multi_f = jnp.where(multi, 1.0, 0.0)
    prefix = jnp.dot(multi_f.astype(BF16), tri_ref[...], preferred_element_type=F32)
    cnt = jnp.broadcast_to(jnp.sum(multi_f, axis=1, keepdims=True), cnt_ref.shape)
    cnt_ref[...] = cnt.astype(I32)
    units = jnp.floor((cnt + (SEG_ALIGN - 1)) * (1.0 / SEG_ALIGN))
    lower = (lax.broadcasted_iota(I32, (N_EXPERTS, N_EXPERTS), 1)
             < lax.broadcasted_iota(I32, (N_EXPERTS, N_EXPERTS), 0))
    seg0 = jnp.dot(jnp.where(lower, 1.0, 0.0).astype(BF16), units.astype(BF16),
                   preferred_element_type=F32)[:, 0:1] * SEG_ALIGN
    pos_ref[...] = jnp.concatenate(
        [jnp.sum(jnp.where(oh, prefix + seg0, 0.0), axis=0, keepdims=True) for oh in onehots],
        axis=0).astype(I32)


def _merge_router(xs, band, ob, gmix, wg, wa, wb, wo, gffn, wr, br):
    d = xs[0].shape[-1]
    tm = TM_DISPATCH
    x_specs, bounds = _group_tile_specs(xs, tm)
    n = bounds[-1] * tm
    tri = jnp.asarray(np.triu(np.ones((tm, tm), np.float32), k=1), BF16)

    def row(width, dil=1):
        return pl.BlockSpec((tm // dil, dil * width), lambda i: (i, 0))

    def const(shape):
        return pl.BlockSpec(shape, lambda i: (0,) * len(shape))

    def tok4():
        return pl.BlockSpec((TOP_K, tm), lambda i: (0, i))

    band_specs = [row(QUAD, dil) for _, dil in DILATED_GROUPS for _ in range(2)]
    band_args = [a for pair in band for a in pair]
    return pl.pallas_call(
        functools.partial(_merge_kernel, bounds=bounds),
        grid=(n // tm,),
        in_specs=x_specs + band_specs + [row(NA_QUADS * QUAD),
                  const((1, d)), const(wg.shape), const(wa.shape), const(wb.shape), const(wo.shape),
                  const((1, d)), const(wr.shape), const(br.shape), const(tri.shape)],
        out_specs=[row(d), row(d), tok4(), tok4(),
                   pl.BlockSpec((None, N_EXPERTS, LANES), lambda i: (i, 0, 0))],
        out_shape=[jax.ShapeDtypeStruct((n, d), F32), jax.ShapeDtypeStruct((n, d), BF16),
                   jax.ShapeDtypeStruct((TOP_K, n), F32), jax.ShapeDtypeStruct((TOP_K, n), I32),
                   jax.ShapeDtypeStruct((n // tm, N_EXPERTS, LANES), I32)],
        scratch_shapes=[pltpu.VMEM((tm, d), BF16)]
                       + [pltpu.VMEM((QUAD // LANES, tm, LANES), F32)] * (2 * (N_DIL - 1)),
        compiler_params=_cparams("parallel"),
    )(*xs, *band_args, ob, gmix, wg, wa, wb, wo, gffn, wr, br, tri)


def _pack_bf16_pair(x):
    w = x.shape[1] // 2
    lo = pltpu.bitcast(x[:, :w], U32) >> 16
    hi = pltpu.bitcast(x[:, w:], U32) & jnp.uint32(0xFFFF0000)
    return lo | hi


def _unpack_bf16_pair(u):
    lo = pltpu.bitcast(u << 16, F32).astype(BF16)
    hi = pltpu.bitcast(u & jnp.uint32(0xFFFF0000), F32).astype(BF16)
    return lo, hi


def _pow2_copies(units, stage0, hbm0, max_units, make_copy, action):
    for b in range(max_units.bit_length() - 1, -1, -1):
        size = SEG_ALIGN << b

        @pl.when(((units >> b) & 1) == 1)
        def _():
            done = ((units >> (b + 1)) << (b + 1)) * SEG_ALIGN
            cp = make_copy(pl.multiple_of(stage0 + done, SEG_ALIGN),
                           pl.multiple_of(hbm0 + done, SEG_ALIGN), size)
            if action == "start":
                cp.start()
            else:
                cp.wait()


def _segment_copies(tile, c8_ref, off_ref, a_ref, make_copy, action, enable=True):
    tile = jnp.maximum(tile, 0)
    for e in range(N_EXPERTS):
        k = tile * N_EXPERTS + e
        units = jnp.where(enable, c8_ref[k] // SEG_ALIGN, 0)
        _pow2_copies(units, off_ref[k], a_ref[k], TM_DISPATCH // SEG_ALIGN, make_copy, action)


def _segment_waits(tile, tot_ref, s_rows, make_copy, enable=True):
    units = jnp.where(enable, tot_ref[jnp.maximum(tile, 0)], 0)
    _pow2_copies(units, 0, 0, s_rows // SEG_ALIGN, make_copy, "wait")


def _dispatch_kernel(c8_ref, off_ref, a_ref, tot_ref, tail_ref, h2_ref, pos_ref, xs_ref, stage_ref, sem):
    tile = pl.program_id(0)
    last = pl.num_programs(0) - 1
    slot = tile % 2
    s_rows, tm = stage_ref.shape[1], h2_ref.shape[0]

    def copier(buf):
        def make_copy(stage_row, hbm_row, size):
            return pltpu.make_async_copy(stage_ref.at[buf, pl.ds(stage_row, size)],
                                         xs_ref.at[pl.ds(hbm_row, size)], sem.at[buf])
        return make_copy

    @pl.when(tile == 0)
    def _():
        stage_ref[0, 0:EXPERT_BM] = jnp.zeros((EXPERT_BM, stage_ref.shape[2]), U32)
        for action in ("start", "wait"):
            for e in range(N_EXPERTS):
                _pow2_copies(tail_ref[N_EXPERTS + e], 0, tail_ref[e], EXPERT_BM // SEG_ALIGN - 1,
                             copier(0), action)

        def zero_block(b, carry):
            cp = copier(0)(0, pl.multiple_of(b * EXPERT_BM, EXPERT_BM), EXPERT_BM)
            cp.start()
            cp.wait()
            return carry
        lax.fori_loop(tail_ref[2 * N_EXPERTS], xs_ref.shape[0] // EXPERT_BM, zero_block, 0)

    _segment_waits(tile - 2, tot_ref, s_rows, copier(slot), enable=tile >= 2)
    _segment_copies(tile - 1, c8_ref, off_ref, a_ref, copier(1 - slot), "start", enable=tile >= 1)

    pos = pos_ref[...]
    h2 = h2_ref[...]
    ck = ONEHOT_CHUNK
    for c in range(s_rows // ck):
        siota = lax.broadcasted_iota(I32, (ck, tm), 0) + c * ck
        compact = jnp.zeros((ck, tm), F32)
        for k in range(TOP_K):
            compact = jnp.where(siota == pos[k:k + 1, :], 1.0, compact)
        rows = jnp.dot(compact.astype(BF16), h2, preferred_element_type=F32)
        stage_ref[slot, c * ck:(c + 1) * ck, :] = _pack_bf16_pair(rows)

    @pl.when(tile == last)
    def _():
        _segment_copies(tile, c8_ref, off_ref, a_ref, copier(slot), "start")
        _segment_waits(tile - 1, tot_ref, s_rows, copier(1 - slot), enable=tile >= 1)
        _segment_waits(tile, tot_ref, s_rows, copier(slot))


def _dispatch(c8, off, a, tot, tail, h2, pos, p_rows, s_rows):
    n, d = h2.shape
    tm = TM_DISPATCH
    grid_spec = pltpu.PrefetchScalarGridSpec(
        num_scalar_prefetch=5,
        grid=(n // tm,),
        in_specs=[pl.BlockSpec((tm, d), lambda i, *_: (i, 0)),
                  pl.BlockSpec((TOP_K, tm), lambda i, *_: (0, i))],
        out_specs=pl.BlockSpec(memory_space=pl.ANY),
        scratch_shapes=[pltpu.VMEM((2, s_rows, d // 2), U32), pltpu.SemaphoreType.DMA((2,))],
    )
    return pl.pallas_call(
        _dispatch_kernel,
        grid_spec=grid_spec,
        out_shape=jax.ShapeDtypeStruct((p_rows, d // 2), U32),
        compiler_params=_cparams("arbitrary"),
    )(c8, off, a, tot, tail, h2, pos)


def _expert_kernel(be_ref, nused_ref, xs_ref, wgu_ref, bgu_ref, wd_ref, bd_ref, ys_ref, wgu_bf, wd_bf):
    b = pl.program_id(0)
    active = b < nused_ref[0]
    new_expert = (b == 0) | (be_ref[b] != be_ref[jnp.maximum(b - 1, 0)])

    @pl.when(jnp.logical_not(active))
    def _():
        ys_ref[...] = jnp.zeros_like(ys_ref)

    @pl.when(active & new_expert)
    def _():
        wgu_bf[...] = wgu_ref[...].astype(BF16)
        wd_bf[...] = wd_ref[...].astype(BF16)

    @pl.when(active)
    def _():
        ck = EXPERT_CHUNK
        de = wd_bf.shape[0]
        rs = xs_ref.shape[0] // EXPERT_ROW_SLICES
        for r in range(EXPERT_ROW_SLICES):
            rows = slice(r * rs, (r + 1) * rs)
            lo, hi = _unpack_bf16_pair(xs_ref[rows, :])
            x = jnp.concatenate([lo, hi], axis=1)
            y = None
            for c in range(de // ck):
                glu_cols, lin_cols = slice(c * ck, (c + 1) * ck), slice(de + c * ck, de + (c + 1) * ck)
                g = jnp.dot(x, wgu_bf[:, glu_cols], preferred_element_type=F32) + bgu_ref[:, glu_cols]
                u = jnp.dot(x, wgu_bf[:, lin_cols], preferred_element_type=F32) + bgu_ref[:, lin_cols]
                x_glu = jnp.minimum(g, SWIGLU_LIMIT)
                x_lin = jnp.clip(u, -SWIGLU_LIMIT, SWIGLU_LIMIT)
                act = x_glu * jax.nn.sigmoid(SWIGLU_ALPHA * x_glu) * (x_lin + 1.0)
                part = jnp.dot(act.astype(BF16), wd_bf[c * ck:(c + 1) * ck, :],
                               preferred_element_type=F32)
                y = part if y is None else y + part
            y = y + bd_ref[...]
            ys_ref[rows, :] = _pack_bf16_pair(y.astype(BF16).astype(F32))


def _expert_ffn(block_e, nused, xs, wgu, bgu, wd, bd):
    p_rows, half = xs.shape
    bm = EXPERT_BM
    d, de2 = wgu.shape[1], wgu.shape[2]

    def blk(b, be, nu):
        return (jnp.minimum(b, nu[0] - 1), 0)

    def wsel(b, be, nu):
        return (be[jnp.minimum(b, nu[0] - 1)], 0, 0)

    grid_spec = pltpu.PrefetchScalarGridSpec(
        num_scalar_prefetch=2,
        grid=(p_rows // bm,),
        in_specs=[pl.BlockSpec((bm, half), blk),
                  pl.BlockSpec((None, d, de2), wsel),
                  pl.BlockSpec((None, 1, de2), wsel),
                  pl.BlockSpec((None, de2 // 2, d), wsel),
                  pl.BlockSpec((None, 1, d), wsel)],
        out_specs=pl.BlockSpec((bm, half), lambda b, be, nu: (b, 0)),
        scratch_shapes=[pltpu.VMEM((d, de2), BF16), pltpu.VMEM((de2 // 2, d), BF16)],
    )
    return pl.pallas_call(
        _expert_kernel,
        grid_spec=grid_spec,
        out_shape=jax.ShapeDtypeStruct((p_rows, half), U32),
        compiler_params=_cparams("arbitrary"),
    )(block_e, nused, xs, wgu, bgu, wd, bd)


def _combine_kernel(c8_ref, off_ref, a_ref, tot_ref, ys_ref, x1_ref, pos_ref, gate_ref, g_ref, o_ref,
                    stage_ref, sem, *, tile0):
    step = pl.program_id(0)
    tile = step + tile0
    slot = step % 2
    s_rows, tm = stage_ref.shape[1], x1_ref.shape[0]

    def copier(buf):
        def make_copy(stage_row, hbm_row, size):
            return pltpu.make_async_copy(ys_ref.at[pl.ds(hbm_row, size)],
                                         stage_ref.at[buf, pl.ds(stage_row, size)], sem.at[buf])
        return make_copy

    @pl.when(step == 0)
    def _():
        stage_ref[...] = jnp.zeros_like(stage_ref)
        _segment_copies(tile, c8_ref, off_ref, a_ref, copier(0), "start")

    @pl.when(step + 1 < pl.num_programs(0))
    def _():
        _segment_copies(tile + 1, c8_ref, off_ref, a_ref, copier(1 - slot), "start")

    _segment_waits(tile, tot_ref, s_rows, copier(slot))
    pos = pos_ref[...]
    gate = gate_ref[...]
    ck = ONEHOT_CHUNK
    moe_lo = moe_hi = None
    for c in range(s_rows // ck):
        liota = lax.broadcasted_iota(I32, (tm, ck), 1) + c * ck
        gmat = jnp.zeros((tm, ck), F32)
        for k in range(TOP_K):
            gmat = jnp.where(liota == pos[:, k:k + 1], gate[:, k:k + 1], gmat)
        gmat = gmat.astype(BF16)
        lo, hi = _unpack_bf16_pair(stage_ref[slot, c * ck:(c + 1) * ck, :])
        part_lo = jnp.dot(gmat, lo, preferred_element_type=F32)
        part_hi = jnp.dot(gmat, hi, preferred_element_type=F32)
        moe_lo = part_lo if moe_lo is None else moe_lo + part_lo
        moe_hi = part_hi if moe_hi is None else moe_hi + part_hi
    moe = jnp.concatenate([moe_lo, moe_hi], axis=1)
    o_ref[...] = _rms(x1_ref[...] + moe, g_ref[...])


def _combine(c8, off, a, tot, ys, x1, pos_t, gate_t, gfinal, tok0, batch, seq, s_rows):
    _, d = x1.shape
    tm = TM_DISPATCH
    tile0, per_seq = tok0 // tm, seq // tm

    def row(width):
        return pl.BlockSpec((tm, width), lambda i, *_: (i + tile0, 0))

    grid_spec = pltpu.PrefetchScalarGridSpec(
        num_scalar_prefetch=4,
        grid=(batch * per_seq,),
        in_specs=[pl.BlockSpec(memory_space=pl.ANY), row(d), row(TOP_K), row(TOP_K),
                  pl.BlockSpec((1, d), lambda i, *_: (0, 0))],
        out_specs=pl.BlockSpec((None, tm, d), lambda i, *_: (i // per_seq, i % per_seq, 0)),
        scratch_shapes=[pltpu.VMEM((2, s_rows, d // 2), U32), pltpu.SemaphoreType.DMA((2,))],
    )
    return pl.pallas_call(
        functools.partial(_combine_kernel, tile0=tile0),
        grid_spec=grid_spec,
        out_shape=jax.ShapeDtypeStruct((batch, seq, d), F32),
        compiler_params=_cparams("arbitrary"),
    )(c8, off, a, tot, ys, x1, pos_t, gate_t, gfinal)


def _routing_tables(cnt, p_rows):
    c8 = _round_up(cnt, SEG_ALIGN)
    off = jnp.cumsum(c8, axis=1) - c8
    tot = jnp.sum(c8, axis=0)
    padded = _round_up(tot, EXPERT_BM)
    pend = jnp.cumsum(padded)
    a = (pend - padded)[None, :] + jnp.cumsum(c8, axis=0) - c8
    n_blocks = p_rows // EXPERT_BM
    block_row = jnp.arange(n_blocks, dtype=I32) * EXPERT_BM
    block_e = jnp.sum((block_row[:, None] >= pend[None, :]).astype(I32), axis=1)
    block_e = jnp.minimum(block_e, N_EXPERTS - 1).astype(I32)
    nused = (pend[-1] // EXPERT_BM).astype(I32).reshape(1)
    tail = jnp.concatenate([pend - padded + tot, (padded - tot) // SEG_ALIGN, nused]).astype(I32)
    tile_units = (jnp.sum(c8, axis=1) // SEG_ALIGN).astype(I32)
    flat = lambda t: t.reshape(-1).astype(I32)
    return flat(c8), flat(off), flat(a), tile_units, tail, block_e, nused


def _permute_qkv_weight(w_in):
    d = w_in.shape[0]
    w = w_in.reshape(d, 3, N_QUADS, QUAD)
    scale = jnp.asarray([HEAD_DIM ** -0.5, 1.0, 1.0], F32).reshape(1, 3, 1, 1)
    return (w * scale).transpose(0, 2, 1, 3).reshape(d, N_QUADS * QKV_W).astype(BF16)


def kernel(x_prompt, x_sample, t5_rel_bias, norm_mix, w_in, na_rpb, w_branch_a, w_branch_b, w_gate, w_out,
           norm_ffn, w_router, b_router, w_gate_up, b_gate_up, w_down, b_down, norm_final):
    assert norm_mix.shape[0] == 1, "single-layer encoder"
    d = x_prompt.shape[-1]
    groups, tok0 = [], 0
    for xg in (x_prompt, x_sample):
        groups.append((xg.shape[0], xg.shape[1], tok0))
        tok0 += xg.shape[0] * xg.shape[1]
    n = tok0
    assert n % TM_QKV == 0 and n % TM_DISPATCH == 0 and all(g[2] % TM_DISPATCH == 0 for g in groups)
    xs_in = (x_prompt, x_sample)

    *qkv_band, qkv_na = _qkv_proj(xs_in, norm_mix[0][None], _permute_qkv_weight(w_in[0]))

    band = []
    for g, (window, dilation) in enumerate(DILATED_GROUPS):
        assert window // (2 * dilation) == BAND_HALF
        bias = _band_bias(t5_rel_bias[:, g * QUAD_HEADS:(g + 1) * QUAD_HEADS], dilation, BAND_BQ)
        band.append(_band_attention(qkv_band[g], dilation, bias, groups))
    ob = _neighbourhood_attention(qkv_na, _na_bias(na_rpb[0]), groups)

    wr = w_router[0].T
    wr_hi = wr.astype(BF16)
    wr_lo = (wr - wr_hi.astype(F32)).astype(BF16)
    x1, h2, gate, pos, cnt = _merge_router(
        xs_in, band, ob, norm_mix[0][None], w_gate[0].astype(BF16), w_branch_a[0].astype(BF16),
        w_branch_b[0].astype(BF16), w_out[0].astype(BF16), norm_ffn[0][None],
        jnp.concatenate([wr_hi, wr_lo], axis=0), b_router[0][:, None])

    n_tiles = n // TM_DISPATCH
    p_rows = _round_up(n * TOP_K + n_tiles * N_EXPERTS * (SEG_ALIGN - 1) + N_EXPERTS * (EXPERT_BM - 1),
                       EXPERT_BM)
    s_rows = _round_up(TM_DISPATCH * TOP_K + N_EXPERTS * (SEG_ALIGN - 1), 256)
    c8, off, a, tot, tail, block_e, nused = _routing_tables(cnt[:, :, 0], p_rows)

    xs = _dispatch(c8, off, a, tot, tail, h2, pos, p_rows, s_rows)
    ys = _expert_ffn(block_e, nused, xs, w_gate_up[0], b_gate_up[0][:, None, :], w_down[0],
                     b_down[0][:, None, :])

    pos_t, gate_t = pos.T, gate.T
    outs = []
    for (b, t, g0) in groups:
        outs.append(_combine(c8, off, a, tot, ys, x1, pos_t, gate_t, norm_final[None], g0, b, t, s_rows))
    return tuple(outs)
```

```python
import functools

import numpy as np
import jax
import jax.numpy as jnp
from jax import lax
from jax.experimental import pallas as pl
from jax.experimental.pallas import tpu as pltpu

F32 = jnp.float32
BF16 = jnp.bfloat16
I32 = jnp.int32
U32 = jnp.uint32

HEAD_DIM = 64
QUAD_HEADS = 4
QUAD = QUAD_HEADS * HEAD_DIM
LANES = 128
QKV_W = 3 * QUAD
DILATED_GROUPS = ((128, 1), (512, 4), (2048, 16))
N_DIL = len(DILATED_GROUPS)
NA_QUADS = 2
N_QUADS = N_DIL + NA_QUADS
BAND_HALF = 64
T5_BUCKETS = 32
T5_MAX_DISTANCE = 1024
GRID_W = 64
NA_KH = 8
NA_KW = 16
N_EXPERTS = 32
TOP_K = 4
SWIGLU_LIMIT = 7.0
SWIGLU_ALPHA = 1.702
RMS_EPS = 1e-6
NEG = -1e30

BAND_BQ = 128
BAND_OUTER = 1024
NA_BAND_ROWS = 8
TM_QKV = 512
TM_DISPATCH = 512
TM_MERGE_SUB = 256
EXPERT_BM = 1024
EXPERT_CHUNK = 256
EXPERT_ROW_SLICES = 2
ONEHOT_CHUNK = 256
SEG_ALIGN = 8
VMEM_LIMIT = 56 * 1024 * 1024


def _round_up(x, m):
    return (x + m - 1) // m * m


def _cparams(*sem):
    return pltpu.CompilerParams(dimension_semantics=sem, vmem_limit_bytes=VMEM_LIMIT)


def _rms(x, g):
    ms = jnp.mean(x * x, axis=-1, keepdims=True)
    return x * lax.rsqrt(ms + RMS_EPS) * g


def _select_group(w, metas, fn):
    out = fn(metas[0], w - metas[0]["base"])
    for m in metas[1:]:
        cand = fn(m, w - m["base"])
        sel = w >= m["base"]
        out = jax.tree.map(lambda a, b: jnp.where(sel, b, a), out, cand)
    return out


def _group_tile_specs(xs, tm):
    specs, bounds, tile0 = [], [], 0
    for xg in xs:
        b, t, d = xg.shape
        assert t % tm == 0
        per_seq, n_tiles = t // tm, b * t // tm

        def im(i, *_, tile0=tile0, per_seq=per_seq, n_tiles=n_tiles):
            j = jnp.clip(i - tile0, 0, n_tiles - 1)
            return (j // per_seq, j % per_seq, 0)
        specs.append(pl.BlockSpec((None, tm, d), im))
        tile0 += n_tiles
        bounds.append(tile0)
    return specs, bounds


def _group_tile(refs, bounds, rows=slice(None)):
    i = pl.program_id(0)
    x = refs[-1][rows, :]
    for ref, bound in zip(refs[-2::-1], bounds[-2::-1]):
        x = jnp.where(i < bound, ref[rows, :], x)
    return x


def _qkv_kernel(*refs, bounds):
    ng = len(bounds)
    x_refs, (g_ref, w_ref), refs = refs[:ng], refs[ng:ng + 2], refs[ng + 2:]
    band_refs, na_ref, h_ref = refs[:N_DIL], refs[N_DIL], refs[N_DIL + 1]
    n_lane_blocks, tm, _ = h_ref.shape
    h_nat = _rms(_group_tile(x_refs, bounds), g_ref[...])
    for c in range(n_lane_blocks):
        h_ref[c] = h_nat[:, c * LANES:(c + 1) * LANES]

    def project(h, j):
        return jnp.dot(h.astype(BF16), w_ref[:, j * QKV_W:(j + 1) * QKV_W],
                       preferred_element_type=F32).astype(BF16)

    def class_rows(r, rows, dil):
        return jnp.concatenate([h_ref[c, pl.ds(r, rows, stride=dil), :] for c in range(n_lane_blocks)],
                               axis=1)

    for j in range(NA_QUADS):
        na_ref[j] = project(h_nat, N_DIL + j)
    for g, (_, dil) in enumerate(DILATED_GROUPS):
        rows = tm // dil
        if dil == 1:
            band_refs[g][...] = project(h_nat, g)
            continue
        hp = jnp.concatenate([class_rows(r, rows, dil) for r in range(dil)], axis=0)
        y = project(hp, g)
        for r in range(dil):
            band_refs[g][:, r * QKV_W:(r + 1) * QKV_W] = y[r * rows:(r + 1) * rows]


def _qkv_proj(xs, g, w):
    d = xs[0].shape[-1]
    tm = TM_QKV
    x_specs, bounds = _group_tile_specs(xs, tm)
    n = bounds[-1] * tm
    band_specs = [pl.BlockSpec((tm // dil, dil * QKV_W), lambda i: (i, 0)) for _, dil in DILATED_GROUPS]
    band_shapes = [jax.ShapeDtypeStruct((n // dil, dil * QKV_W), BF16) for _, dil in DILATED_GROUPS]
    return pl.pallas_call(
        functools.partial(_qkv_kernel, bounds=bounds),
        grid=(n // tm,),
        in_specs=x_specs + [pl.BlockSpec((1, d), lambda i: (0, 0)),
                            pl.BlockSpec((d, N_QUADS * QKV_W), lambda i: (0, 0))],
        out_specs=band_specs + [pl.BlockSpec((NA_QUADS, tm, QKV_W), lambda i: (0, i, 0))],
        out_shape=band_shapes + [jax.ShapeDtypeStruct((NA_QUADS, n, QKV_W), BF16)],
        scratch_shapes=[pltpu.VMEM((d // LANES, tm, LANES), F32)],
        compiler_params=_cparams("parallel"),
    )(*xs, g, w)


def _head_lane_masks(rows):
    lane_head = lax.broadcasted_iota(I32, (rows, QUAD), 1) // HEAD_DIM
    return [lane_head == h for h in range(QUAD_HEADS)]


def _stack_heads(q, masks):
    zero = jnp.zeros_like(q)
    return jnp.concatenate([jnp.where(m, q, zero) for m in masks], axis=0)


def _band_kernel(cur_ref, prev_ref, next_ref, b_ref, o_ref, l_ref, *, decode, bq):
    bo = cur_ref.shape[0]
    nc, n_sub = cur_ref.shape[1] // QKV_W, bo // bq
    t = decode(pl.program_id(0))
    first = (t["i"] == 0).astype(I32)
    last = (t["i"] == t["last"]).astype(I32)
    masks = _head_lane_masks(bq)
    for c in range(nc):
        qcol, kcol, vcol = (slice(c * QKV_W + j * QUAD, c * QKV_W + (j + 1) * QUAD) for j in range(3))
        k = jnp.concatenate([prev_ref[:, kcol], cur_ref[:, kcol], next_ref[:, kcol]], axis=0)
        v = jnp.concatenate([prev_ref[:, vcol], cur_ref[:, vcol], next_ref[:, vcol]], axis=0)
        ocol = slice(c * QUAD, (c + 1) * QUAD)
        for j in range(n_sub):
            rows = slice(j * bq, (j + 1) * bq)
            keys = slice(j * bq, (j + 1) * bq + 2 * BAND_HALF)
            variant = (first if j == 0 else 0) + (2 * last if j == n_sub - 1 else 0)
            qs = _stack_heads(cur_ref[rows, qcol], masks)
            s = lax.dot_general(qs, k[keys], (((1,), (1,)), ((), ())), preferred_element_type=F32)
            s = s + b_ref[variant]
            m = jnp.max(s, axis=-1, keepdims=True)
            p = jnp.exp(s - m)
            l = jnp.sum(p, axis=-1, keepdims=True)
            pv = jnp.dot(p.astype(BF16), v[keys], preferred_element_type=F32) * (1.0 / l)
            lse = jnp.broadcast_to(m + jnp.log(l), pv.shape)
            o = jnp.zeros((bq, QUAD), F32)
            le = jnp.zeros((bq, QUAD), F32)
            for h in range(QUAD_HEADS):
                o = jnp.where(masks[h], pv[h * bq:(h + 1) * bq], o)
                le = jnp.where(masks[h], lse[h * bq:(h + 1) * bq], le)
            o_ref[rows, ocol] = o
            l_ref[rows, ocol] = le


def _t5_bucket(rel):
    nb = T5_BUCKETS // 2
    ret = (rel > 0).astype(np.int32) * nb
    n = np.abs(rel)
    max_exact = nb // 2
    large = max_exact + (np.log(np.maximum(n, 1) / max_exact)
                         / np.log(T5_MAX_DISTANCE / max_exact) * (nb - max_exact)).astype(np.int32)
    large = np.minimum(large, nb - 1)
    return ret + np.where(n < max_exact, n, large)


def _select_rows(table, index, n_rows):
    onehot = np.zeros((index.size, n_rows), np.float32)
    onehot[np.arange(index.size), index.reshape(-1)] = 1.0
    out = jnp.einsum("nb,b...->n...", jnp.asarray(onehot), table.astype(F32),
                     precision=lax.Precision.HIGHEST)
    return out.reshape(index.shape + table.shape[1:])


def _band_bias(table, dilation, bq):
    nk = bq + 2 * BAND_HALF
    qi = np.arange(bq)[:, None]
    c = np.arange(nk)[None, :]
    rel = c - BAND_HALF - qi
    band = np.abs(rel) <= BAND_HALF
    bias = jnp.transpose(_select_rows(table, _t5_bucket(dilation * rel), T5_BUCKETS), (2, 0, 1))
    out = []
    for variant in range(4):
        valid = band
        if variant & 1:
            valid = valid & (c >= BAND_HALF)
        if variant & 2:
            valid = valid & (c < bq + BAND_HALF)
        out.append(jnp.where(valid[None], bias, NEG).reshape(QUAD_HEADS * bq, nk))
    return jnp.stack(out, axis=0)


def _band_attention(a, dilation, bias, groups):
    d, bq = dilation, BAND_BQ
    n = a.shape[0] * d
    bo = min([BAND_OUTER] + [t // d for (_, t, _) in groups])
    nc = min(d, BAND_OUTER // bo)
    hq = bo // BAND_HALF
    metas, base = [], 0
    for (b, t, tok0) in groups:
        seq = t // d
        assert bo % bq == 0 and seq % bo == 0 and tok0 % (d * bo) == 0 and d % nc == 0
        nb = seq // bo
        metas.append(dict(base=base, nb=nb, rb0=tok0 // d // bo))
        base += b * (d // nc) * nb
    n_items = base

    def decode(w):
        def f(m, wl):
            nb = m["nb"]
            s = wl // ((d // nc) * nb)
            r = (wl // nb) % (d // nc)
            i = wl % nb
            seq0 = m["rb0"] + s * nb
            return dict(r=r, i=i, seq0=seq0, last=nb - 1 + 0 * i)
        return _select_group(w, metas, f)

    def cur_im(w):
        t = decode(w)
        return (t["seq0"] + t["i"], t["r"])

    def prev_im(w):
        t = decode(w)
        return (hq * t["seq0"] + jnp.maximum(hq * t["i"] - 1, 0), t["r"])

    def next_im(w):
        t = decode(w)
        return (hq * t["seq0"] + jnp.minimum(hq * t["i"] + hq, hq * t["last"] + hq - 1), t["r"])

    return pl.pallas_call(
        functools.partial(_band_kernel, decode=decode, bq=bq),
        grid=(n_items,),
        in_specs=[pl.BlockSpec((bo, nc * QKV_W), cur_im),
                  pl.BlockSpec((BAND_HALF, nc * QKV_W), prev_im),
                  pl.BlockSpec((BAND_HALF, nc * QKV_W), next_im),
                  pl.BlockSpec(bias.shape, lambda w: (0, 0, 0))],
        out_specs=[pl.BlockSpec((bo, nc * QUAD), cur_im), pl.BlockSpec((bo, nc * QUAD), cur_im)],
        out_shape=[jax.ShapeDtypeStruct((n // d, d * QUAD), F32)] * 2,
        compiler_params=_cparams("parallel"),
    )(a, a, a, bias)


def _na_bias(rpb):
    qc = np.arange(GRID_W)[:, None]
    kc = np.arange(GRID_W)[None, :]
    col_idx = np.clip(kc - qc + NA_KW - 1, 0, 2 * NA_KW - 2)
    ws = np.clip(qc - NA_KW // 2, 0, GRID_W - NA_KW)
    mask = (kc >= ws) & (kc < ws + NA_KW)
    t = _select_rows(jnp.moveaxis(rpb, 2, 0), col_idx, 2 * NA_KW - 1)
    t = jnp.where(mask[:, :, None, None], t, NEG)
    b = jnp.stack([t[..., dv:dv + NA_KH] for dv in range(NA_KH)], axis=0)
    b = b.reshape(NA_KH, GRID_W, GRID_W, NA_QUADS, QUAD_HEADS, NA_KH)
    b = b.transpose(3, 0, 4, 1, 5, 2)
    return b.reshape(NA_QUADS, NA_KH, QUAD_HEADS * GRID_W, NA_KH * GRID_W)


def _na_kernel(q_ref, kp_ref, kc_ref, kn_ref, vp_ref, vc_ref, vn_ref, b_ref, o_ref,
               kbuf, vbuf, *, metas, items_per_quad):
    blk = NA_BAND_ROWS * GRID_W
    w = pl.program_id(0) % items_per_quad
    t = _select_group(w, metas, lambda m, wl: dict(i=wl % m["nbands"], nbands=m["nbands"] + 0 * wl))
    band, rows = t["i"], t["nbands"] * NA_BAND_ROWS
    kbuf[0:blk] = kp_ref[...]
    kbuf[blk:2 * blk] = kc_ref[...]
    kbuf[2 * blk:3 * blk] = kn_ref[...]
    vbuf[0:blk] = vp_ref[...]
    vbuf[blk:2 * blk] = vc_ref[...]
    vbuf[2 * blk:3 * blk] = vn_ref[...]
    masks = _head_lane_masks(GRID_W)

    for j in range(NA_BAND_ROWS):
        r = band * NA_BAND_ROWS + j
        rs = jnp.clip(r - NA_KH // 2, 0, rows - NA_KH)
        loc = pl.multiple_of((rs - (band - 1) * NA_BAND_ROWS) * GRID_W, GRID_W)
        kw = kbuf[pl.ds(loc, NA_KH * GRID_W), :]
        vw = vbuf[pl.ds(loc, NA_KH * GRID_W), :]
        qrow = slice(j * GRID_W, (j + 1) * GRID_W)
        qs = _stack_heads(q_ref[qrow, :], masks)
        s = lax.dot_general(qs, kw, (((1,), (1,)), ((), ())), preferred_element_type=F32)
        s = s + b_ref[rs - r + NA_KH - 1]
        m = jnp.max(s, axis=-1, keepdims=True)
        p = jnp.exp(s - m)
        l = jnp.sum(p, axis=-1, keepdims=True)
        pv = jnp.dot(p.astype(BF16), vw, preferred_element_type=F32) * (1.0 / l)
        o = jnp.zeros((GRID_W, QUAD), F32)
        for h in range(QUAD_HEADS):
            o = jnp.where(masks[h], pv[h * GRID_W:(h + 1) * GRID_W], o)
        o_ref[qrow, :] = o


def _neighbourhood_attention(qkv, bias, groups):
    _, n, _ = qkv.shape
    blk = NA_BAND_ROWS * GRID_W
    metas, base = [], 0
    for (b, t, tok0) in groups:
        rows = t // GRID_W
        assert t % GRID_W == 0 and rows % NA_BAND_ROWS == 0 and rows >= NA_KH and tok0 % blk == 0
        nbands = rows // NA_BAND_ROWS
        metas.append(dict(base=base, nbands=nbands, tb0=tok0 // blk))
        base += b * nbands
    items_per_quad = base

    def decode(w):
        quad = w // items_per_quad
        wq = w % items_per_quad

        def f(m, wl):
            s = wl // m["nbands"]
            i = wl % m["nbands"]
            return dict(i=i, seq0=m["tb0"] + s * m["nbands"], last=m["nbands"] - 1 + 0 * i)
        t = _select_group(wq, metas, f)
        t["quad"] = quad
        return t

    def spec(col, shift):
        def im(w):
            t = decode(w)
            i = jnp.clip(t["i"] + shift, 0, t["last"])
            return (t["quad"], t["seq0"] + i, col)
        return pl.BlockSpec((None, blk, QUAD), im)

    def out_im(w):
        t = decode(w)
        return (t["seq0"] + t["i"], t["quad"])

    return pl.pallas_call(
        functools.partial(_na_kernel, metas=metas, items_per_quad=items_per_quad),
        grid=(NA_QUADS * items_per_quad,),
        in_specs=[spec(0, 0), spec(1, -1), spec(1, 0), spec(1, 1), spec(2, -1), spec(2, 0), spec(2, 1),
                  pl.BlockSpec((None, NA_KH, QUAD_HEADS * GRID_W, NA_KH * GRID_W),
                               lambda w: (w // items_per_quad, 0, 0, 0))],
        out_specs=pl.BlockSpec((blk, QUAD), out_im),
        out_shape=jax.ShapeDtypeStruct((n, NA_QUADS * QUAD), F32),
        scratch_shapes=[pltpu.VMEM((3 * blk, QUAD), BF16), pltpu.VMEM((3 * blk, QUAD), BF16)],
        compiler_params=_cparams("parallel"),
    )(qkv, qkv, qkv, qkv, qkv, qkv, qkv, bias)


def _merge_kernel(*refs, bounds):
    ng = len(bounds)
    x_refs, refs = refs[:ng], refs[ng:]
    (o0_ref, l0_ref, o1_ref, l1_ref, o2_ref, l2_ref, ob_ref,
     gmix_ref, wg_ref, wa_ref, wb_ref, wo_ref, gffn_ref, wr_ref, br_ref, tri_ref,
     x1_ref, h2_ref, gate_ref, pos_ref, cnt_ref, h2lo_ref, *nat_refs) = refs
    tm, d = x1_ref.shape
    ts = TM_MERGE_SUB

    band = [(o0_ref, l0_ref)]
    pairs = ((o1_ref, l1_ref), (o2_ref, l2_ref))
    for g, (_, dil) in enumerate(DILATED_GROUPS[1:]):
        nat = nat_refs[2 * g:2 * g + 2]
        for src, dst in zip(pairs[g], nat):
            for r in range(dil):
                for c in range(QUAD // LANES):
                    col = r * QUAD + c * LANES
                    dst[c, pl.ds(r, tm // dil, stride=dil), :] = src[:, col:col + LANES]
        band.append(nat)

    for sub in range(tm // ts):
        rows = slice(sub * ts, (sub + 1) * ts)
        x = _group_tile(x_refs, bounds, rows)
        h = _rms(x, gmix_ref[...]).astype(BF16)
        gates = jax.nn.sigmoid(jnp.dot(h, wg_ref[...], preferred_element_type=F32))

        def rows_of(ref):
            if len(ref.shape) == 2:
                return ref[rows, :]
            return jnp.concatenate([ref[c, rows, :] for c in range(ref.shape[0])], axis=1)

        (o0, l0), (o1, l1), (o2, l2) = [(rows_of(o), rows_of(l)) for o, l in band]
        m = jnp.maximum(jnp.maximum(l0, l1), l2)
        e0, e1, e2 = jnp.exp(l0 - m), jnp.exp(l1 - m), jnp.exp(l2 - m)
        oa = (e0 * o0 + e1 * o1 + e2 * o2) * (1.0 / (e0 + e1 + e2))

        bra = jnp.dot(oa.astype(BF16), wa_ref[...], preferred_element_type=F32)
        brb = jnp.dot(ob_ref[rows, :].astype(BF16), wb_ref[...], preferred_element_type=F32)
        merged = (gates[:, :d] * bra + gates[:, d:] * brb).astype(BF16)
        x1 = x + jnp.dot(merged, wo_ref[...], preferred_element_type=F32)
        x1_ref[rows, :] = x1
        h2 = _rms(x1, gffn_ref[...])
        h2_hi = h2.astype(BF16)
        h2_ref[rows, :] = h2_hi
        h2lo_ref[rows, :] = (h2 - h2_hi.astype(F32)).astype(BF16)

    nt = (((1,), (1,)), ((), ()))
    hi_terms = lax.dot_general(wr_ref[...], h2_ref[...], nt, preferred_element_type=F32)
    logits = (hi_terms[:N_EXPERTS] + hi_terms[N_EXPERTS:]
              + lax.dot_general(wr_ref[:N_EXPERTS, :], h2lo_ref[...], nt, preferred_element_type=F32)
              + br_ref[...])
    eiota = lax.broadcasted_iota(I32, (N_EXPERTS, tm), 0)
    idxs, vals = [], []
    for _ in range(TOP_K):
        top = jnp.max(logits, axis=0, keepdims=True)
        idx = jnp.min(jnp.where(logits == top, eiota, N_EXPERTS), axis=0, keepdims=True)
        idxs.append(idx)
        vals.append(top)
        logits = jnp.where(eiota == idx, -jnp.inf, logits)
    ex = [jnp.exp(v - vals[0]) for v in vals]
    inv = 1.0 / (ex[0] + ex[1] + ex[2] + ex[3])
    gate_ref[...] = jnp.concatenate([v * inv for v in ex], axis=0)

    onehots = [eiota == idx for idx in idxs]
    multi = onehots[0] | onehots[1] | onehots[2] | onehots[3]
    multi_f = jnp.where(multi, 1.0, 0.0)
    prefix = jnp.dot(multi_f.astype(BF16), tri_ref[...], preferred_element_type=F32)
    cnt = jnp.broadcast_to(jnp.sum(multi_f, axis=1, keepdims=True), cnt_ref.shape)
    cnt_ref[...] = cnt.astype(I32)
    units = jnp.floor((cnt + (SEG_ALIGN - 1)) * (1.0 / SEG_ALIGN))
    lower = (lax.broadcasted_iota(I32, (N_EXPERTS, N_EXPERTS), 1)
             < lax.broadcasted_iota(I32, (N_EXPERTS, N_EXPERTS), 0))
    seg0 = jnp.dot(jnp.where(lower, 1.0, 0.0).astype(BF16), units.astype(BF16),
                   preferred_element_type=F32)[:, 0:1] * SEG_ALIGN
    pos_ref[...] = jnp.concatenate(
        [jnp.sum(jnp.where(oh, prefix + seg0, 0.0), axis=0, keepdims=True) for oh in onehots],
        axis=0).astype(I32)


def _merge_router(xs, band, ob, gmix, wg, wa, wb, wo, gffn, wr, br):
    d = xs[0].shape[-1]
    tm = TM_DISPATCH
    x_specs, bounds = _group_tile_specs(xs, tm)
    n = bounds[-1] * tm
    tri = jnp.asarray(np.triu(np.ones((tm, tm), np.float32), k=1), BF16)

    def row(width, dil=1):
        return pl.BlockSpec((tm // dil, dil * width), lambda i: (i, 0))

    def const(shape):
        return pl.BlockSpec(shape, lambda i: (0,) * len(shape))

    def tok4():
        return pl.BlockSpec((TOP_K, tm), lambda i: (0, i))

    band_specs = [row(QUAD, dil) for _, dil in DILATED_GROUPS for _ in range(2)]
    band_args = [a for pair in band for a in pair]
    return pl.pallas_call(
        functools.partial(_merge_kernel, bounds=bounds),
        grid=(n // tm,),
        in_specs=x_specs + band_specs + [row(NA_QUADS * QUAD),
                  const((1, d)), const(wg.shape), const(wa.shape), const(wb.shape), const(wo.shape),
                  const((1, d)), const(wr.shape), const(br.shape), const(tri.shape)],
        out_specs=[row(d), row(d), tok4(), tok4(),
                   pl.BlockSpec((None, N_EXPERTS, LANES), lambda i: (i, 0, 0))],
        out_shape=[jax.ShapeDtypeStruct((n, d), F32), jax.ShapeDtypeStruct((n, d), BF16),
                   jax.ShapeDtypeStruct((TOP_K, n), F32), jax.ShapeDtypeStruct((TOP_K, n), I32),
                   jax.ShapeDtypeStruct((n // tm, N_EXPERTS, LANES), I32)],
        scratch_shapes=[pltpu.VMEM((tm, d), BF16)]
                       + [pltpu.VMEM((QUAD // LANES, tm, LANES), F32)] * (2 * (N_DIL - 1)),
        compiler_params=_cparams("parallel"),
    )(*xs, *band_args, ob, gmix, wg, wa, wb, wo, gffn, wr, br, tri)


def _pack_bf16_pair(x):
    w = x.shape[1] // 2
    lo = pltpu.bitcast(x[:, :w], U32) >> 16
    hi = pltpu.bitcast(x[:, w:], U32) & jnp.uint32(0xFFFF0000)
    return lo | hi


def _unpack_bf16_pair(u):
    lo = pltpu.bitcast(u << 16, F32).astype(BF16)
    hi = pltpu.bitcast(u & jnp.uint32(0xFFFF0000), F32).astype(BF16)
    return lo, hi


def _pow2_copies(units, stage0, hbm0, max_units, make_copy, action):
    for b in range(max_units.bit_length() - 1, -1, -1):
        size = SEG_ALIGN << b

        @pl.when(((units >> b) & 1) == 1)
        def _():
            done = ((units >> (b + 1)) << (b + 1)) * SEG_ALIGN
            cp = make_copy(pl.multiple_of(stage0 + done, SEG_ALIGN),
                           pl.multiple_of(hbm0 + done, SEG_ALIGN), size)
            if action == "start":
                cp.start()
            else:
                cp.wait()


def _segment_copies(tile, c8_ref, off_ref, a_ref, make_copy, action, enable=True):
    tile = jnp.maximum(tile, 0)
    for e in range(N_EXPERTS):
        k = tile * N_EXPERTS + e
        units = jnp.where(enable, c8_ref[k] // SEG_ALIGN, 0)
        _pow2_copies(units, off_ref[k], a_ref[k], TM_DISPATCH // SEG_ALIGN, make_copy, action)


def _segment_waits(tile, tot_ref, s_rows, make_copy, enable=True):
    units = jnp.where(enable, tot_ref[jnp.maximum(tile, 0)], 0)
    _pow2_copies(units, 0, 0, s_rows // SEG_ALIGN, make_copy, "wait")


def _dispatch_kernel(c8_ref, off_ref, a_ref, tot_ref, tail_ref, h2_ref, pos_ref, xs_ref, stage_ref, sem):
    tile = pl.program_id(0)
    last = pl.num_programs(0) - 1
    slot = tile % 2
    s_rows, tm = stage_ref.shape[1], h2_ref.shape[0]

    def copier(buf):
        def make_copy(stage_row, hbm_row, size):
            return pltpu.make_async_copy(stage_ref.at[buf, pl.ds(stage_row, size)],
                                         xs_ref.at[pl.ds(hbm_row, size)], sem.at[buf])
        return make_copy

    @pl.when(tile == 0)
    def _():
        stage_ref[0, 0:EXPERT_BM] = jnp.zeros((EXPERT_BM, stage_ref.shape[2]), U32)
        for action in ("start", "wait"):
            for e in range(N_EXPERTS):
                _pow2_copies(tail_ref[N_EXPERTS + e], 0, tail_ref[e], EXPERT_BM // SEG_ALIGN - 1,
                             copier(0), action)

        def zero_block(b, carry):
            cp = copier(0)(0, pl.multiple_of(b * EXPERT_BM, EXPERT_BM), EXPERT_BM)
            cp.start()
            cp.wait()
            return carry
        lax.fori_loop(tail_ref[2 * N_EXPERTS], xs_ref.shape[0] // EXPERT_BM, zero_block, 0)

    _segment_waits(tile - 2, tot_ref, s_rows, copier(slot), enable=tile >= 2)
    _segment_copies(tile - 1, c8_ref, off_ref, a_ref, copier(1 - slot), "start", enable=tile >= 1)

    pos = pos_ref[...]
    h2 = h2_ref[...]
    ck = ONEHOT_CHUNK
    for c in range(s_rows // ck):
        siota = lax.broadcasted_iota(I32, (ck, tm), 0) + c * ck
        compact = jnp.zeros((ck, tm), F32)
        for k in range(TOP_K):
            compact = jnp.where(siota == pos[k:k + 1, :], 1.0, compact)
        rows = jnp.dot(compact.astype(BF16), h2, preferred_element_type=F32)
        stage_ref[slot, c * ck:(c + 1) * ck, :] = _pack_bf16_pair(rows)

    @pl.when(tile == last)
    def _():
        _segment_copies(tile, c8_ref, off_ref, a_ref, copier(slot), "start")
        _segment_waits(tile - 1, tot_ref, s_rows, copier(1 - slot), enable=tile >= 1)
        _segment_waits(tile, tot_ref, s_rows, copier(slot))


def _dispatch(c8, off, a, tot, tail, h2, pos, p_rows, s_rows):
    n, d = h2.shape
    tm = TM_DISPATCH
    grid_spec = pltpu.PrefetchScalarGridSpec(
        num_scalar_prefetch=5,
        grid=(n // tm,),
        in_specs=[pl.BlockSpec((tm, d), lambda i, *_: (i, 0)),
                  pl.BlockSpec((TOP_K, tm), lambda i, *_: (0, i))],
        out_specs=pl.BlockSpec(memory_space=pl.ANY),
        scratch_shapes=[pltpu.VMEM((2, s_rows, d // 2), U32), pltpu.SemaphoreType.DMA((2,))],
    )
    return pl.pallas_call(
        _dispatch_kernel,
        grid_spec=grid_spec,
        out_shape=jax.ShapeDtypeStruct((p_rows, d // 2), U32),
        compiler_params=_cparams("arbitrary"),
    )(c8, off, a, tot, tail, h2, pos)


def _expert_kernel(be_ref, nused_ref, xs_ref, wgu_ref, bgu_ref, wd_ref, bd_ref, ys_ref, wgu_bf, wd_bf):
    b = pl.program_id(0)
    active = b < nused_ref[0]
    new_expert = (b == 0) | (be_ref[b] != be_ref[jnp.maximum(b - 1, 0)])

    @pl.when(jnp.logical_not(active))
    def _():
        ys_ref[...] = jnp.zeros_like(ys_ref)

    @pl.when(active & new_expert)
    def _():
        wgu_bf[...] = wgu_ref[...].astype(BF16)
        wd_bf[...] = wd_ref[...].astype(BF16)

    @pl.when(active)
    def _():
        ck = EXPERT_CHUNK
        de = wd_bf.shape[0]
        rs = xs_ref.shape[0] // EXPERT_ROW_SLICES
        for r in range(EXPERT_ROW_SLICES):
            rows = slice(r * rs, (r + 1) * rs)
            lo, hi = _unpack_bf16_pair(xs_ref[rows, :])
            x = jnp.concatenate([lo, hi], axis=1)
            y = None
            for c in range(de // ck):
                glu_cols, lin_cols = slice(c * ck, (c + 1) * ck), slice(de + c * ck, de + (c + 1) * ck)
                g = jnp.dot(x, wgu_bf[:, glu_cols], preferred_element_type=F32) + bgu_ref[:, glu_cols]
                u = jnp.dot(x, wgu_bf[:, lin_cols], preferred_element_type=F32) + bgu_ref[:, lin_cols]
                x_glu = jnp.minimum(g, SWIGLU_LIMIT)
                x_lin = jnp.clip(u, -SWIGLU_LIMIT, SWIGLU_LIMIT)
                act = x_glu * jax.nn.sigmoid(SWIGLU_ALPHA * x_glu) * (x_lin + 1.0)
                part = jnp.dot(act.astype(BF16), wd_bf[c * ck:(c + 1) * ck, :],
                               preferred_element_type=F32)
                y = part if y is None else y + part
            y = y + bd_ref[...]
            ys_ref[rows, :] = _pack_bf16_pair(y.astype(BF16).astype(F32))


def _expert_ffn(block_e, nused, xs, wgu, bgu, wd, bd):
    p_rows, half = xs.shape
    bm = EXPERT_BM
    d, de2 = wgu.shape[1], wgu.shape[2]

    def blk(b, be, nu):
        return (jnp.minimum(b, nu[0] - 1), 0)

    def wsel(b, be, nu):
        return (be[jnp.minimum(b, nu[0] - 1)], 0, 0)

    grid_spec = pltpu.PrefetchScalarGridSpec(
        num_scalar_prefetch=2,
        grid=(p_rows // bm,),
        in_specs=[pl.BlockSpec((bm, half), blk),
                  pl.BlockSpec((None, d, de2), wsel),
                  pl.BlockSpec((None, 1, de2), wsel),
                  pl.BlockSpec((None, de2 // 2, d), wsel),
                  pl.BlockSpec((None, 1, d), wsel)],
        out_specs=pl.BlockSpec((bm, half), lambda b, be, nu: (b, 0)),
        scratch_shapes=[pltpu.VMEM((d, de2), BF16), pltpu.VMEM((de2 // 2, d), BF16)],
    )
    return pl.pallas_call(
        _expert_kernel,
        grid_spec=grid_spec,
        out_shape=jax.ShapeDtypeStruct((p_rows, half), U32),
        compiler_params=_cparams("arbitrary"),
    )(block_e, nused, xs, wgu, bgu, wd, bd)


def _combine_kernel(c8_ref, off_ref, a_ref, tot_ref, ys_ref, x1_ref, pos_ref, gate_ref, g_ref, o_ref,
                    stage_ref, sem, *, tile0):
    step = pl.program_id(0)
    tile = step + tile0
    slot = step % 2
    s_rows, tm = stage_ref.shape[1], x1_ref.shape[0]

    def copier(buf):
        def make_copy(stage_row, hbm_row, size):
            return pltpu.make_async_copy(ys_ref.at[pl.ds(hbm_row, size)],
                                         stage_ref.at[buf, pl.ds(stage_row, size)], sem.at[buf])
        return make_copy

    @pl.when(step == 0)
    def _():
        stage_ref[...] = jnp.zeros_like(stage_ref)
        _segment_copies(tile, c8_ref, off_ref, a_ref, copier(0), "start")

    @pl.when(step + 1 < pl.num_programs(0))
    def _():
        _segment_copies(tile + 1, c8_ref, off_ref, a_ref, copier(1 - slot), "start")

    _segment_waits(tile, tot_ref, s_rows, copier(slot))
    pos = pos_ref[...]
    gate = gate_ref[...]
    ck = ONEHOT_CHUNK
    moe_lo = moe_hi = None
    for c in range(s_rows // ck):
        liota = lax.broadcasted_iota(I32, (tm, ck), 1) + c * ck
        gmat = jnp.zeros((tm, ck), F32)
        for k in range(TOP_K):
            gmat = jnp.where(liota == pos[:, k:k + 1], gate[:, k:k + 1], gmat)
        gmat = gmat.astype(BF16)
        lo, hi = _unpack_bf16_pair(stage_ref[slot, c * ck:(c + 1) * ck, :])
        part_lo = jnp.dot(gmat, lo, preferred_element_type=F32)
        part_hi = jnp.dot(gmat, hi, preferred_element_type=F32)
        moe_lo = part_lo if moe_lo is None else moe_lo + part_lo
        moe_hi = part_hi if moe_hi is None else moe_hi + part_hi
    moe = jnp.concatenate([moe_lo, moe_hi], axis=1)
    o_ref[...] = _rms(x1_ref[...] + moe, g_ref[...])


def _combine(c8, off, a, tot, ys, x1, pos_t, gate_t, gfinal, tok0, batch, seq, s_rows):
    _, d = x1.shape
    tm = TM_DISPATCH
    tile0, per_seq = tok0 // tm, seq // tm

    def row(width):
        return pl.BlockSpec((tm, width), lambda i, *_: (i + tile0, 0))

    grid_spec = pltpu.PrefetchScalarGridSpec(
        num_scalar_prefetch=4,
        grid=(batch * per_seq,),
        in_specs=[pl.BlockSpec(memory_space=pl.ANY), row(d), row(TOP_K), row(TOP_K),
                  pl.BlockSpec((1, d), lambda i, *_: (0, 0))],
        out_specs=pl.BlockSpec((None, tm, d), lambda i, *_: (i // per_seq, i % per_seq, 0)),
        scratch_shapes=[pltpu.VMEM((2, s_rows, d // 2), U32), pltpu.SemaphoreType.DMA((2,))],
    )
    return pl.pallas_call(
        functools.partial(_combine_kernel, tile0=tile0),
        grid_spec=grid_spec,
        out_shape=jax.ShapeDtypeStruct((batch, seq, d), F32),
        compiler_params=_cparams("arbitrary"),
    )(c8, off, a, tot, ys, x1, pos_t, gate_t, gfinal)


def _routing_tables(cnt, p_rows):
    c8 = _round_up(cnt, SEG_ALIGN)
    off = jnp.cumsum(c8, axis=1) - c8
    tot = jnp.sum(c8, axis=0)
    padded = _round_up(tot, EXPERT_BM)
    pend = jnp.cumsum(padded)
    a = (pend - padded)[None, :] + jnp.cumsum(c8, axis=0) - c8
    n_blocks = p_rows // EXPERT_BM
    block_row = jnp.arange(n_blocks, dtype=I32) * EXPERT_BM
    block_e = jnp.sum((block_row[:, None] >= pend[None, :]).astype(I32), axis=1)
    block_e = jnp.minimum(block_e, N_EXPERTS - 1).astype(I32)
    nused = (pend[-1] // EXPERT_BM).astype(I32).reshape(1)
    tail = jnp.concatenate([pend - padded + tot, (padded - tot) // SEG_ALIGN, nused]).astype(I32)
    tile_units = (jnp.sum(c8, axis=1) // SEG_ALIGN).astype(I32)
    flat = lambda t: t.reshape(-1).astype(I32)
    return flat(c8), flat(off), flat(a), tile_units, tail, block_e, nused


def _permute_qkv_weight(w_in):
    d = w_in.shape[0]
    w = w_in.reshape(d, 3, N_QUADS, QUAD)
    scale = jnp.asarray([HEAD_DIM ** -0.5, 1.0, 1.0], F32).reshape(1, 3, 1, 1)
    return (w * scale).transpose(0, 2, 1, 3).reshape(d, N_QUADS * QKV_W).astype(BF16)


def kernel(x_prompt, x_sample, t5_rel_bias, norm_mix, w_in, na_rpb, w_branch_a, w_branch_b, w_gate, w_out,
           norm_ffn, w_router, b_router, w_gate_up, b_gate_up, w_down, b_down, norm_final):
    assert norm_mix.shape[0] == 1, "single-layer encoder"
    d = x_prompt.shape[-1]
    groups, tok0 = [], 0
    for xg in (x_prompt, x_sample):
        groups.append((xg.shape[0], xg.shape[1], tok0))
        tok0 += xg.shape[0] * xg.shape[1]
    n = tok0
    assert n % TM_QKV == 0 and n % TM_DISPATCH == 0 and all(g[2] % TM_DISPATCH == 0 for g in groups)
    xs_in = (x_prompt, x_sample)

    *qkv_band, qkv_na = _qkv_proj(xs_in, norm_mix[0][None], _permute_qkv_weight(w_in[0]))

    band = []
    for g, (window, dilation) in enumerate(DILATED_GROUPS):
        assert window // (2 * dilation) == BAND_HALF
        bias = _band_bias(t5_rel_bias[:, g * QUAD_HEADS:(g + 1) * QUAD_HEADS], dilation, BAND_BQ)
        band.append(_band_attention(qkv_band[g], dilation, bias, groups))
    ob = _neighbourhood_attention(qkv_na, _na_bias(na_rpb[0]), groups)

    wr = w_router[0].T
    wr_hi = wr.astype(BF16)
    wr_lo = (wr - wr_hi.astype(F32)).astype(BF16)
    x1, h2, gate, pos, cnt = _merge_router(
        xs_in, band, ob, norm_mix[0][None], w_gate[0].astype(BF16), w_branch_a[0].astype(BF16),
        w_branch_b[0].astype(BF16), w_out[0].astype(BF16), norm_ffn[0][None],
        jnp.concatenate([wr_hi, wr_lo], axis=0), b_router[0][:, None])

    n_tiles = n // TM_DISPATCH
    p_rows = _round_up(n * TOP_K + n_tiles * N_EXPERTS * (SEG_ALIGN - 1) + N_EXPERTS * (EXPERT_BM - 1),
                       EXPERT_BM)
    s_rows = _round_up(TM_DISPATCH * TOP_K + N_EXPERTS * (SEG_ALIGN - 1), 256)
    c8, off, a, tot, tail, block_e, nused = _routing_tables(cnt[:, :, 0], p_rows)

    xs = _dispatch(c8, off, a, tot, tail, h2, pos, p_rows, s_rows)
    ys = _expert_ffn(block_e, nused, xs, w_gate_up[0], b_gate_up[0][:, None, :], w_down[0],
                     b_down[0][:, None, :])

    pos_t, gate_t = pos.T, gate.T
    outs = []
    for (b, t, g0) in groups:
        outs.append(_combine(c8, off, a, tot, ys, x1, pos_t, gate_t, norm_final[None], g0, b, t, s_rows))
    return tuple(outs)
```
